```python
import math, functools
import jax, jax.numpy as jnp
from jax import lax
import numpy as np

D_MODEL = 1024
BATCH = 4
SEQ = 8192
DEPTH = 2
DEC_BATCH = 32
DEC_SEQ = 1
PAST_LEN = 16384
PAGE_SIZE = 128

HD = 64
H_A = D_MODEL // (2 * HD)
KV_A = 2
G_A = H_A // KV_A
H_I = 8
D_IDX = 64
TOPK_MAX = 256
H_B = D_MODEL // (4 * HD)
MIX_W = H_A * HD + H_B * 2 * HD
N_MEM = 256
MEM_H = 4
_FF_RAW = -(-8 * D_MODEL // 3)
D_FF = -(-_FF_RAW // 256) * 256
NUM_BUCKETS = 32
MAX_EXACT = 16
MAX_DIST = 128
QBLK = 128
EPS = 1e-6
NEG = -1e30
IN_WIDTHS = (H_A * HD, KV_A * HD, KV_A * HD, H_I * D_IDX, D_IDX, H_I, H_B * 2 * HD, H_B * 2 * HD, H_B * 2 * HD)
W_IN = sum(IN_WIDTHS)

kernel_name = "hybrid_dsa_diffattn_decoder_step"


def rms_norm(x, g):
    xf = x.astype(jnp.float32)
    y = xf * lax.rsqrt(jnp.mean(xf * xf, axis=-1, keepdims=True) + EPS)
    return (y * g.astype(jnp.float32)).astype(x.dtype)


def t5_bucket(dist):
    n = jnp.maximum(dist, 0)
    nf = jnp.maximum(n, 1).astype(jnp.float32)
    large = MAX_EXACT + (jnp.log(nf / MAX_EXACT) / math.log(MAX_DIST / MAX_EXACT)
                         * (NUM_BUCKETS - MAX_EXACT)).astype(jnp.int32)
    large = jnp.minimum(large, NUM_BUCKETS - 1)
    return jnp.where(n < MAX_EXACT, n, large)


def mix_inputs(h, w_in_l, g_qa_l, g_ka_l, g_qb_l, g_kb_l):
    n, t, _ = h.shape
    qa, ka, va, iq, ik, iw, qb, kb, vb = jnp.split(h @ w_in_l, np.cumsum(IN_WIDTHS)[:-1].tolist(), axis=-1)
    qa = rms_norm(qa.reshape(n, t, H_A, HD), g_qa_l)
    ka = rms_norm(ka.reshape(n, t, KV_A, HD), g_ka_l)
    va = va.reshape(n, t, KV_A, HD)
    iq = iq.reshape(n, t, H_I, D_IDX)
    iw = iw * H_I ** -0.5
    qb = rms_norm(qb.reshape(n, t, H_B, 2, HD), g_qb_l)
    kb = rms_norm(kb.reshape(n, t, H_B, 2, HD), g_kb_l)
    vb = vb.reshape(n, t, H_B, 2 * HD)
    return qa, ka, va, iq, ik, iw, qb, kb, vb


def index_scores(iq, iw, ik, qpos, kpos):
    dots = jnp.einsum('nthd,nsd->nths', iq, ik).astype(jnp.float32) * D_IDX ** -0.5
    s = jnp.einsum('nths,nth->nts', jax.nn.relu(dots), iw.astype(jnp.float32))
    return jnp.where(kpos[None, None, :] <= qpos[None, :, None], s, NEG)


def dsa_attend(q, k_sel, v_sel, sel, qpos, tbl_a):
    n, t = q.shape[:2]
    dist = qpos[None, :, None] - sel
    bias = tbl_a[t5_bucket(dist)]
    bias = bias.reshape(n, t, -1, KV_A, G_A).transpose(0, 1, 3, 4, 2)
    qg = q.reshape(n, t, KV_A, G_A, HD)
    logits = jnp.einsum('ntgrd,ntkgd->ntgrk', qg, k_sel).astype(jnp.float32) * HD ** -0.5 + bias
    logits = jnp.where((dist >= 0)[:, :, None, None, :], logits, NEG)
    p = jax.nn.softmax(logits, axis=-1).astype(v_sel.dtype)
    o = jnp.einsum('ntgrk,ntkgd->ntgrd', p, v_sel)
    return o.reshape(n, t, H_A * HD)


def to_blocks(a):
    n, s = a.shape[:2]
    return jnp.moveaxis(a.reshape(n, s // QBLK, QBLK, *a.shape[2:]), 1, 0)


def from_blocks(a):
    a = jnp.moveaxis(a, 0, 1)
    return a.reshape(a.shape[0], a.shape[1] * a.shape[2], *a.shape[3:])


def dsa_prompt(qa, ka, va, iq, iw, ik, tbl_a):
    n, s = qa.shape[:2]
    k_top = min(TOPK_MAX, s // 4)
    pos = jnp.arange(s, dtype=jnp.int32)
    bidx = jnp.arange(n)[:, None, None]

    def block(args):
        q_blk, iq_blk, iw_blk, qpos = args
        _, sel = lax.top_k(index_scores(iq_blk, iw_blk, ik, qpos, pos), k_top)
        return dsa_attend(q_blk, ka[bidx, sel], va[bidx, sel], sel, qpos, tbl_a)

    out = lax.map(block, (to_blocks(qa), to_blocks(iq), to_blocks(iw), pos.reshape(s // QBLK, QBLK)))
    return from_blocks(out)


def dsa_sample(l, qa, ka, va, iq, iw, ik, cache_a_k, cache_a_v, cache_a_ik, page_table, tbl_a):
    n, t = qa.shape[:2]
    n_keys = PAST_LEN + t
    k_top = min(TOPK_MAX, n_keys // 4)
    qpos = PAST_LEN + jnp.arange(t, dtype=jnp.int32)
    kpos = jnp.arange(n_keys, dtype=jnp.int32)
    ik_past = cache_a_ik[l, page_table].reshape(n, PAST_LEN, D_IDX)
    ik_all = jnp.concatenate([ik_past, ik.astype(ik_past.dtype)], axis=1)
    _, sel = lax.top_k(index_scores(iq, iw, ik_all, qpos, kpos), k_top)
    bidx = jnp.arange(n)[:, None, None]
    sp = jnp.minimum(sel, PAST_LEN - 1)
    phys = page_table[bidx, sp // PAGE_SIZE]
    slot = sp % PAGE_SIZE
    sn = jnp.clip(sel - PAST_LEN, 0, t - 1)
    is_past = (sel < PAST_LEN)[..., None, None]
    k_sel = jnp.where(is_past, cache_a_k[l, phys, slot], ka[bidx, sn])
    v_sel = jnp.where(is_past, cache_a_v[l, phys, slot], va[bidx, sn])
    return dsa_attend(qa, k_sel, v_sel, sel, qpos, tbl_a)


def diff_logits(q, k, qpos, kpos, tbl_b):
    lg = jnp.einsum('nthcd,nshcd->nhcts', q, k).astype(jnp.float32) * HD ** -0.5
    dist = qpos[:, None] - kpos[None, :]
    bias = jnp.transpose(tbl_b[t5_bucket(dist)], (2, 0, 1))
    lg = lg + bias[None, :, None]
    return jnp.where((dist >= 0)[None, None, None], lg, NEG)


def diff_finish(logits, v_segs, lam_l, lam_init, g_sub_l):
    n, _, _, t, _ = logits.shape
    p = jax.nn.softmax(logits, axis=-1)
    a = p[:, :, 0] - lam_l * p[:, :, 1]
    offs = np.cumsum([0] + [v.shape[1] for v in v_segs]).tolist()
    outs = [jnp.einsum('nhts,nshe->nthe', a[..., offs[i]:offs[i + 1]].astype(v.dtype), v)
            for i, v in enumerate(v_segs)]
    o = functools.reduce(jnp.add, outs)
    o = rms_norm(o, g_sub_l) * (1.0 - lam_init)
    return o.reshape(n, t, H_B * 2 * HD)


def diff_prompt(qb, kb, vb, tbl_b, lam_l, lam_init, g_sub_l):
    s = qb.shape[1]
    pos = jnp.arange(s, dtype=jnp.int32)

    def block(args):
        q_blk, qpos = args
        return diff_finish(diff_logits(q_blk, kb, qpos, pos, tbl_b), [vb], lam_l, lam_init, g_sub_l)

    return from_blocks(lax.map(block, (to_blocks(qb), pos.reshape(s // QBLK, QBLK))))


def diff_sample(l, qb, kb, vb, cache_b_k, cache_b_v, page_table, tbl_b, lam_l, lam_init, g_sub_l):
    n, t = qb.shape[:2]
    qpos = PAST_LEN + jnp.arange(t, dtype=jnp.int32)
    k_past = cache_b_k[l, page_table].reshape(n, PAST_LEN, H_B, 2, HD)
    v_past = cache_b_v[l, page_table].reshape(n, PAST_LEN, H_B, 2 * HD)
    lg = jnp.concatenate([diff_logits(qb, k_past, qpos, jnp.arange(PAST_LEN, dtype=jnp.int32), tbl_b),
                          diff_logits(qb, kb, qpos, qpos, tbl_b)], axis=-1)
    return diff_finish(lg, [v_past, vb], lam_l, lam_init, g_sub_l)


def mem_kv(mem, g_mem_l, w_mkv_l, g_mk_l):
    n = mem.shape[0]
    mk, mv = jnp.split(rms_norm(mem, g_mem_l) @ w_mkv_l, 2, axis=-1)
    mk = rms_norm(mk.reshape(n, N_MEM, MEM_H, HD), g_mk_l)
    return mk, mv.reshape(n, N_MEM, MEM_H, HD)


def mem_attend(h, mk, mv, w_mq_l, g_mq_l, w_mo_l):
    n, t, _ = h.shape
    q = rms_norm((h @ w_mq_l).reshape(n, t, MEM_H, HD), g_mq_l)
    lg = jnp.einsum('nthd,nshd->nhts', q, mk).astype(jnp.float32) * HD ** -0.5
    p = jax.nn.softmax(lg, axis=-1).astype(mv.dtype)
    o = jnp.einsum('nhts,nshd->nthd', p, mv).reshape(n, t, MEM_H * HD)
    return o @ w_mo_l


def swiglu(h, w_gu_l, w_down_l):
    g, u = jnp.split(h @ w_gu_l, 2, axis=-1)
    return (jax.nn.silu(g) * u) @ w_down_l


def setup_inputs(seed: int = 0) -> dict:
    key = jax.random.key(seed)
    ks = jax.random.split(key, 32)
    f32 = jnp.float32
    n_pages = PAST_LEN // PAGE_SIZE
    n_used = DEC_BATCH * n_pages
    n_pool = n_used + max(1, n_used // 4)

    def nrm(k, shape, scale=1.0):
        return jax.random.normal(k, shape, f32) * scale

    def gain(k, shape):
        return 1.0 + nrm(k, shape, 0.05)

    page_table = jax.random.permutation(ks[9], n_pool)[:n_used].reshape(DEC_BATCH, n_pages).astype(jnp.int32)
    return {
        "x_prompt": nrm(ks[0], (BATCH, SEQ, D_MODEL)),
        "x_sample": nrm(ks[1], (DEC_BATCH, DEC_SEQ, D_MODEL)),
        "cache_a_k": nrm(ks[2], (DEPTH, n_pool, PAGE_SIZE, KV_A, HD)),
        "cache_a_v": nrm(ks[3], (DEPTH, n_pool, PAGE_SIZE, KV_A, HD)),
        "cache_a_ik": nrm(ks[4], (DEPTH, n_pool, PAGE_SIZE, D_IDX)),
        "cache_b_k": nrm(ks[5], (DEPTH, n_pool, PAGE_SIZE, H_B, 2 * HD)),
        "cache_b_v": nrm(ks[6], (DEPTH, n_pool, PAGE_SIZE, H_B, 2 * HD)),
        "cache_mem_k": nrm(ks[7], (DEPTH, DEC_BATCH, N_MEM, MEM_H, HD)),
        "cache_mem_v": nrm(ks[8], (DEPTH, DEC_BATCH, N_MEM, MEM_H, HD)),
        "page_table": page_table,
        "mem_prompt": nrm(ks[10], (BATCH, N_MEM, D_MODEL)),
        "rel_bias": nrm(ks[11], (NUM_BUCKETS, H_A + H_B), 0.5),
        "g_mix": gain(ks[12], (DEPTH, D_MODEL)),
        "w_in": nrm(ks[13], (DEPTH, D_MODEL, W_IN), D_MODEL ** -0.5),
        "g_qa": gain(ks[14], (DEPTH, HD)),
        "g_ka": gain(ks[15], (DEPTH, HD)),
        "g_qb": gain(ks[16], (DEPTH, HD)),
        "g_kb": gain(ks[17], (DEPTH, HD)),
        "lam": nrm(ks[18], (DEPTH, 4, HD), 0.1),
        "g_sub": gain(ks[19], (DEPTH, 2 * HD)),
        "w_o": nrm(ks[20], (DEPTH, MIX_W, D_MODEL), MIX_W ** -0.5),
        "g_cross": gain(ks[21], (DEPTH, D_MODEL)),
        "g_mem": gain(ks[22], (DEPTH, D_MODEL)),
        "w_mq": nrm(ks[23], (DEPTH, D_MODEL, MEM_H * HD), D_MODEL ** -0.5),
        "w_mkv": nrm(ks[24], (DEPTH, D_MODEL, 2 * MEM_H * HD), D_MODEL ** -0.5),
        "g_mq": gain(ks[25], (DEPTH, HD)),
        "g_mk": gain(ks[26], (DEPTH, HD)),
        "w_mo": nrm(ks[27], (DEPTH, MEM_H * HD, D_MODEL), (MEM_H * HD) ** -0.5),
        "g_ffn": gain(ks[28], (DEPTH, D_MODEL)),
        "w_gu": nrm(ks[29], (DEPTH, D_MODEL, 2 * D_FF), D_MODEL ** -0.5),
        "w_down": nrm(ks[30], (DEPTH, D_FF, D_MODEL), D_FF ** -0.5),
    }


def reference(x_prompt, x_sample, cache_a_k, cache_a_v, cache_a_ik, cache_b_k, cache_b_v,
              cache_mem_k, cache_mem_v, page_table, mem_prompt, rel_bias, g_mix, w_in,
              g_qa, g_ka, g_qb, g_kb, lam, g_sub, w_o, g_cross, g_mem, w_mq, w_mkv,
              g_mq, g_mk, w_mo, g_ffn, w_gu, w_down):
    tbl_a = rel_bias[:, :H_A]
    tbl_b = rel_bias[:, H_A:]
    yp, ys = x_prompt, x_sample
    rows_p = [[] for _ in range(7)]
    rows_s = [[] for _ in range(5)]
    for l in range(DEPTH):
        lam_init = 0.8 - 0.6 * math.exp(-0.3 * l)
        lf = lam[l].astype(jnp.float32)
        lam_l = jnp.exp(jnp.sum(lf[0] * lf[1])) - jnp.exp(jnp.sum(lf[2] * lf[3])) + lam_init

        qa, ka, va, iq, ik, iw, qb, kb, vb = mix_inputs(rms_norm(yp, g_mix[l]), w_in[l], g_qa[l], g_ka[l], g_qb[l], g_kb[l])
        o = jnp.concatenate([dsa_prompt(qa, ka, va, iq, iw, ik, tbl_a),
                             diff_prompt(qb, kb, vb, tbl_b, lam_l, lam_init, g_sub[l])], axis=-1)
        yp = yp + o @ w_o[l]
        mk, mv = mem_kv(mem_prompt, g_mem[l], w_mkv[l], g_mk[l])
        yp = yp + mem_attend(rms_norm(yp, g_cross[l]), mk, mv, w_mq[l], g_mq[l], w_mo[l])
        yp = yp + swiglu(rms_norm(yp, g_ffn[l]), w_gu[l], w_down[l])
        n, t = kb.shape[:2]
        for i, r in enumerate((ka, va, ik, kb.reshape(n, t, H_B, 2 * HD), vb, mk, mv)):
            rows_p[i].append(r)

        qa, ka, va, iq, ik, iw, qb, kb, vb = mix_inputs(rms_norm(ys, g_mix[l]), w_in[l], g_qa[l], g_ka[l], g_qb[l], g_kb[l])
        o = jnp.concatenate([dsa_sample(l, qa, ka, va, iq, iw, ik, cache_a_k, cache_a_v, cache_a_ik, page_table, tbl_a),
                             diff_sample(l, qb, kb, vb, cache_b_k, cache_b_v, page_table, tbl_b, lam_l, lam_init, g_sub[l])], axis=-1)
        ys = ys + o @ w_o[l]
        ys = ys + mem_attend(rms_norm(ys, g_cross[l]), cache_mem_k[l], cache_mem_v[l], w_mq[l], g_mq[l], w_mo[l])
        ys = ys + swiglu(rms_norm(ys, g_ffn[l]), w_gu[l], w_down[l])
        n, t = kb.shape[:2]
        for i, r in enumerate((ka, va, ik, kb.reshape(n, t, H_B, 2 * HD), vb)):
            rows_s[i].append(r)

    return (yp, ys,
            jnp.stack(rows_p[0]), jnp.stack(rows_p[1]), jnp.stack(rows_p[2]), jnp.stack(rows_p[3]),
            jnp.stack(rows_p[4]), jnp.stack(rows_p[5]), jnp.stack(rows_p[6]),
            jnp.stack(rows_s[0]), jnp.stack(rows_s[1]), jnp.stack(rows_s[2]), jnp.stack(rows_s[3]),
            jnp.stack(rows_s[4]))
```

```python
import functools
import math

import numpy as np
import jax
import jax.numpy as jnp
from jax import lax
from jax.experimental import pallas as pl
from jax.experimental.pallas import tpu as pltpu

F32 = jnp.float32
BF16 = jnp.bfloat16
I32 = jnp.int32

HD = 64
H_A = 8
KV_A = 2
G_A = H_A // KV_A
H_I = 8
D_IDX = 64
H_B = 4
MEM_H = 4
TOPK_MAX = 256
PAGE = 128
NUM_BUCKETS = 32
MAX_EXACT = 16
MAX_DIST = 128
EPS = 1e-6
NEG = -1e30

TILE = 256
SAMPLE_ROWS = 16
PAGES_PER_STEP = 8
V7X_VMEM_LIMIT = 56 * 1024 * 1024
INT_MIN = -2 ** 31


def _cparams(n_axes):
    return pltpu.CompilerParams(dimension_semantics=("arbitrary",) * n_axes,
                                vmem_limit_bytes=V7X_VMEM_LIMIT)


def _t5_bucket(dist):
    n = jnp.maximum(dist, 0)
    nf = jnp.maximum(n, 1).astype(F32)
    large = MAX_EXACT + (jnp.log(nf / MAX_EXACT) / math.log(MAX_DIST / MAX_EXACT)
                         * (NUM_BUCKETS - MAX_EXACT)).astype(I32)
    large = jnp.minimum(large, NUM_BUCKETS - 1)
    return jnp.where(n < MAX_EXACT, n, large)


def _rms_rows(x, g_row):
    ms = jnp.mean(x * x, axis=1, keepdims=True)
    return x * lax.rsqrt(ms + EPS) * g_row


def _proj_kernel(*refs, segs, hi_prec, tm):
    if hi_prec:
        x_ref, g_ref, w_ref, wlo_ref, gcol_ref = refs[:5]
        out_refs = refs[5:]
    else:
        x_ref, g_ref, w_ref, gcol_ref = refs[:4]
        wlo_ref = None
        out_refs = refs[4:]
    xn = _rms_rows(x_ref[...], g_ref[...])
    xt = xn.T
    xt_hi = xt.astype(BF16)
    if hi_prec:
        xt_lo = (xt - xt_hi.astype(F32)).astype(BF16)
    oi = 0
    for (r0, nr, norm, outs) in segs:
        p = jnp.dot(w_ref[r0:r0 + nr, :], xt_hi, preferred_element_type=F32)
        if hi_prec:
            p = p + jnp.dot(wlo_ref[r0:r0 + nr, :], xt_hi, preferred_element_type=F32)
            p = p + jnp.dot(w_ref[r0:r0 + nr, :], xt_lo, preferred_element_type=F32)
        if norm:
            p3 = p.reshape(nr // HD, HD, tm)
            ss = jnp.sum(p3 * p3, axis=1, keepdims=True)
            p = (p3 * lax.rsqrt(ss * (1.0 / HD) + EPS)).reshape(nr, tm)
        p = p * gcol_ref[r0:r0 + nr, :]
        pt = None
        for (kind, off, n) in outs:
            o_ref = out_refs[oi]
            oi += 1
            if kind[0] == "T":
                o_ref[...] = p[off:off + n, :].astype(o_ref.dtype)
            else:
                if pt is None:
                    pt = p.T
                o_ref[...] = pt[:, off:off + n].astype(o_ref.dtype)


def _proj_call(x2d, g, w_t, gcol, segs, tm, w_lo=None):
    t_tot, d = x2d.shape
    r_tot = w_t.shape[0]
    nt = t_tot // tm
    hi_prec = w_lo is not None
    in_specs = [pl.BlockSpec((tm, d), lambda j: (j, 0)),
                pl.BlockSpec((1, d), lambda j: (0, 0)),
                pl.BlockSpec((r_tot, d), lambda j: (0, 0))]
    args = [x2d, g.reshape(1, d).astype(F32), w_t]
    if hi_prec:
        in_specs.append(pl.BlockSpec((r_tot, d), lambda j: (0, 0)))
        args.append(w_lo)
    in_specs.append(pl.BlockSpec((r_tot, 1), lambda j: (0, 0)))
    args.append(gcol)
    out_shapes, out_specs = [], []
    for (_, _, _, outs) in segs:
        for (kind, _, n) in outs:
            dt = BF16 if kind.endswith("16") else F32
            if kind[0] == "T":
                out_shapes.append(jax.ShapeDtypeStruct((nt, n, tm), dt))
                out_specs.append(pl.BlockSpec((None, n, tm), lambda j: (j, 0, 0)))
            else:
                out_shapes.append(jax.ShapeDtypeStruct((t_tot, n), dt))
                out_specs.append(pl.BlockSpec((tm, n), lambda j: (j, 0)))
    return pl.pallas_call(
        functools.partial(_proj_kernel, segs=tuple(segs), hi_prec=hi_prec, tm=tm),
        grid=(nt,), in_specs=in_specs, out_specs=out_specs, out_shape=out_shapes,
        compiler_params=_cparams(1), name="proj")(*args)


_SEG_QA, _SEG_IQ, _SEG_QB, _SEG_KA, _SEG_KB, _SEG_VA, _SEG_VB, _SEG_IK = range(8)
_IN_SEGS = (
    (0, 512, True, (("T16", 0, 512),)),
    (512, 512, False, (("T16", 0, 512),)),
    (1024, 512, True, (("T16", 0, 512),)),
    (1536, 128, True, (("R32", 0, 128), ("R16", 0, 128))),
    (1664, 512, True, (("R32", 0, 512), ("R16", 0, 512))),
    (2176, 128, False, (("R32", 0, 128), ("T16", 0, 128))),
    (2304, 512, False, (("R32", 0, 512), ("T16", 0, 512))),
    (2816, 128, False, (("R32", 0, 64), ("R16", 0, 64), ("T32", 64, 16))),
)
_IN_ROWS = 2944


def _in_proj_weights(w_in_l, g_qa_l, g_ka_l, g_qb_l, g_kb_l):
    d = w_in_l.shape[0]
    widths = (H_A * HD, KV_A * HD, KV_A * HD, H_I * D_IDX, D_IDX, H_I, H_B * 2 * HD, H_B * 2 * HD, H_B * 2 * HD)
    offs = np.cumsum((0,) + widths)
    qa, ka, va, iq, ik, iw, qb, kb, vb = [w_in_l[:, offs[i]:offs[i + 1]] for i in range(9)]
    pad = jnp.zeros((d, 128 - D_IDX - H_I), w_in_l.dtype)
    w = jnp.concatenate([qa, iq, qb, ka, kb, va, vb, ik, iw, pad], axis=1)
    one = lambda n: jnp.ones((n,), F32)
    idx_scale = (H_I ** -0.5) * (D_IDX ** -0.5)
    gcol = jnp.concatenate([
        jnp.tile(g_qa_l.astype(F32), H_A) * (HD ** -0.5), one(512),
        jnp.tile(g_qb_l.astype(F32), 2 * H_B) * (HD ** -0.5),
        jnp.tile(g_ka_l.astype(F32), KV_A), jnp.tile(g_kb_l.astype(F32), 2 * H_B),
        one(128), one(512), one(D_IDX), one(H_I) * idx_scale, jnp.zeros((128 - D_IDX - H_I,), F32)])
    return w.T.astype(BF16), gcol.reshape(-1, 1)


def _key_to_f32(key):
    bits = jnp.where(key >= 0, key, key ^ jnp.int32(0x7FFFFFFF))
    return pltpu.bitcast(bits, F32)


def _dsa_kernel(qa_ref, iq_ref, iw_ref, ik_ref, ka_ref, va_ref, bias_ref, o_ref,
                sc_ref, qpad_ref, m_ref, l_ref, acc_ref, sel_ref, *, topk, nbits):
    tq = TILE
    tk = TILE
    i = pl.program_id(1)
    nch = i + 1
    rows = lax.broadcasted_iota(I32, (tk, tq), 0)
    cols = lax.broadcasted_iota(I32, (tk, tq), 1)

    iw = iw_ref[...]

    def score_chunk(c):
        kc = ik_ref[pl.ds(pl.multiple_of(c * tk, tk), tk), :]
        acc = jnp.zeros((tk, tq), F32)
        for h in range(H_I):
            d = jnp.dot(kc, iq_ref[h * D_IDX:(h + 1) * D_IDX, :], preferred_element_type=F32)
            acc = acc + jnp.maximum(d, 0.0) * iw[h:h + 1, :]
        return acc

    def far_scores(c, carry):
        sc_ref[pl.ds(pl.multiple_of(c * tk, tk), tk), :] = score_chunk(c)
        return carry

    lax.fori_loop(0, i, far_scores, 0)
    sc_ref[pl.ds(pl.multiple_of(i * tk, tk), tk), :] = jnp.where(rows <= cols, score_chunk(i), NEG)

    qpos = i * tq + lax.broadcasted_iota(I32, (1, tq), 1)
    kf = jnp.minimum(qpos + 1, topk).astype(F32)

    def count(pred):
        def body(c, acc):
            r0 = pl.multiple_of(c * tk, tk)
            m = jnp.where(pred(sc_ref[pl.ds(r0, tk), :], c), 1.0, 0.0)
            return acc + jnp.sum(m.reshape(tk // 8, 8, tq), axis=0)
        acc = lax.fori_loop(0, nch, body, jnp.zeros((8, tq), F32))
        return jnp.sum(acc, axis=0, keepdims=True)

    def bit_step(it, carry):
        prefix, cge = carry
        cand = prefix | jnp.left_shift(jnp.int32(1), 31 - it)
        thr = _key_to_f32(cand ^ jnp.int32(INT_MIN))
        cnt = count(lambda blk, c: blk >= thr)
        take = cnt >= kf
        return jnp.where(take, cand, prefix), jnp.where(take, cnt, cge)

    prefix, cge = lax.fori_loop(0, 32, bit_step, (jnp.zeros((1, tq), I32), jnp.zeros((1, tq), F32)))
    tau = _key_to_f32(prefix ^ jnp.int32(INT_MIN))

    sel_ref[...] = jnp.full((1, tq), 2 ** 30, I32)
    any_excess = jnp.max(jnp.where(cge > kf, 1.0, 0.0)) > 0.0

    @pl.when(any_excess)
    def _():
        need = kf - count(lambda blk, c: blk > tau)

        def idx_step(it, jlim):
            cand = jlim | jnp.left_shift(jnp.int32(1), nbits - 1 - it)
            cnt = count(lambda blk, c: (blk == tau) & ((c * tk + rows) < cand))
            return jnp.where(cnt < need, cand, jlim)

        sel_ref[...] = lax.fori_loop(0, nbits, idx_step, jnp.zeros((1, tq), I32))

    jlim = sel_ref[...]

    qpad_ref[...] = jnp.zeros(qpad_ref.shape, BF16)
    for h in range(H_A):
        g = h // G_A
        qpad_ref[h, g * HD:(g + 1) * HD, :] = qa_ref[h * HD:(h + 1) * HD, :]
    m_ref[...] = jnp.full(m_ref.shape, NEG, F32)
    l_ref[...] = jnp.zeros(l_ref.shape, F32)
    acc_ref[...] = jnp.zeros(acc_ref.shape, F32)

    def attend(c, kind):
        r0 = pl.multiple_of(c * tk, tk)
        blk = sc_ref[pl.ds(r0, tk), :]
        sel = (blk > tau) | ((blk == tau) & ((c * tk + rows) <= jlim))
        kc = ka_ref[pl.ds(r0, tk), :]
        vt = va_ref[c]
        for h in range(H_A):
            g = h // G_A
            s = jnp.dot(kc, qpad_ref[h], preferred_element_type=F32)
            if kind is not None:
                s = s + bias_ref[h, kind]
            s = jnp.where(sel, s, NEG)
            m_old = m_ref[h:h + 1, :]
            m_new = jnp.maximum(m_old, jnp.max(s, axis=0, keepdims=True))
            alpha = jnp.exp(m_old - m_new)
            p = jnp.exp(s - m_new)
            l_ref[h:h + 1, :] = alpha * l_ref[h:h + 1, :] + jnp.sum(p, axis=0, keepdims=True)
            m_ref[h:h + 1, :] = m_new
            pv = jnp.dot(vt[g * HD:(g + 1) * HD, :], p.astype(BF16), preferred_element_type=F32)
            acc_ref[h] = acc_ref[h] * alpha + pv

    def far_attend(c, carry):
        attend(c, None)
        return carry

    lax.fori_loop(0, jnp.maximum(i - 1, 0), far_attend, 0)

    @pl.when(i >= 1)
    def _():
        attend(i - 1, 0)

    attend(i, 1)

    out = jnp.concatenate([acc_ref[h] / l_ref[h:h + 1, :] for h in range(H_A)], axis=0)
    o_ref[...] = out.T.astype(o_ref.dtype)


def _dsa_prompt_call(qa_t, iq_t, iw_t, ik16, ka16, va_t, bias, n, s):
    nq = s // TILE
    topk = min(TOPK_MAX, s // 4)
    nbits = max(1, int(math.ceil(math.log2(s))))
    return pl.pallas_call(
        functools.partial(_dsa_kernel, topk=topk, nbits=nbits),
        grid=(n, nq),
        in_specs=[
            pl.BlockSpec((None, H_A * HD, TILE), lambda b, i: (b * nq + i, 0, 0)),
            pl.BlockSpec((None, H_I * D_IDX, TILE), lambda b, i: (b * nq + i, 0, 0)),
            pl.BlockSpec((None, 16, TILE), lambda b, i: (b * nq + i, 0, 0)),
            pl.BlockSpec((None, s, D_IDX), lambda b, i: (b, 0, 0)),
            pl.BlockSpec((None, s, KV_A * HD), lambda b, i: (b, 0, 0)),
            pl.BlockSpec((None, nq, KV_A * HD, TILE), lambda b, i: (b, 0, 0, 0)),
            pl.BlockSpec((H_A, 2, TILE, TILE), lambda b, i: (0, 0, 0, 0)),
        ],
        out_specs=pl.BlockSpec((TILE, H_A * HD), lambda b, i: (b * nq + i, 0)),
        out_shape=jax.ShapeDtypeStruct((n * s, H_A * HD), BF16),
        scratch_shapes=[
            pltpu.VMEM((s, TILE), F32),
            pltpu.VMEM((H_A, KV_A * HD, TILE), BF16),
            pltpu.VMEM((H_A, TILE), F32),
            pltpu.VMEM((H_A, TILE), F32),
            pltpu.VMEM((H_A, HD, TILE), F32),
            pltpu.VMEM((1, TILE), I32),
        ],
        compiler_params=_cparams(2), name="dsa_prompt",
    )(qa_t, iq_t, iw_t, ik16.reshape(n, s, D_IDX), ka16.reshape(n, s, KV_A * HD),
      va_t.reshape(n, nq, KV_A * HD, TILE), bias)


def _diff_kernel(qb_ref, kb_ref, vb_ref, bias_ref, lam_ref, gsub_ref, o_ref,
                 qpad_ref, m_ref, l_ref, acc_ref):
    tq = TILE
    tk = TILE
    i = pl.program_id(1)
    rows = lax.broadcasted_iota(I32, (tk, tq), 0)
    cols = lax.broadcasted_iota(I32, (tk, tq), 1)
    nmap = 2 * H_B

    qpad_ref[...] = jnp.zeros(qpad_ref.shape, BF16)
    for mp in range(nmap):
        c = mp % 2
        qpad_ref[mp, c * HD:(c + 1) * HD, :] = qb_ref[mp * HD:(mp + 1) * HD, :]
    m_ref[...] = jnp.full(m_ref.shape, NEG, F32)
    l_ref[...] = jnp.zeros(l_ref.shape, F32)
    acc_ref[...] = jnp.zeros(acc_ref.shape, F32)

    def attend(c, kind):
        r0 = pl.multiple_of(c * tk, tk)
        vt = vb_ref[c]
        for h in range(H_B):
            kc = kb_ref[pl.ds(r0, tk), h * 2 * HD:(h + 1) * 2 * HD]
            vh = vt[h * 2 * HD:(h + 1) * 2 * HD, :]
            for cm in range(2):
                mp = 2 * h + cm
                s = jnp.dot(kc, qpad_ref[mp], preferred_element_type=F32)
                if kind is not None:
                    s = s + bias_ref[h, kind]
                if kind == 1:
                    s = jnp.where(rows <= cols, s, NEG)
                m_old = m_ref[mp:mp + 1, :]
                m_new = jnp.maximum(m_old, jnp.max(s, axis=0, keepdims=True))
                alpha = jnp.exp(m_old - m_new)
                p = jnp.exp(s - m_new)
                l_ref[mp:mp + 1, :] = alpha * l_ref[mp:mp + 1, :] + jnp.sum(p, axis=0, keepdims=True)
                m_ref[mp:mp + 1, :] = m_new
                pv = jnp.dot(vh, p.astype(BF16), preferred_element_type=F32)
                acc_ref[mp] = acc_ref[mp] * alpha + pv

    def far_attend(c, carry):
        attend(c, None)
        return carry

    lax.fori_loop(0, jnp.maximum(i - 1, 0), far_attend, 0)

    @pl.when(i >= 1)
    def _():
        attend(i - 1, 0)

    attend(i, 1)

    lam = lam_ref[...]
    outs = []
    for h in range(H_B):
        o = acc_ref[2 * h] / l_ref[2 * h:2 * h + 1, :] - lam * (acc_ref[2 * h + 1] / l_ref[2 * h + 1:2 * h + 2, :])
        ms = jnp.mean(o * o, axis=0, keepdims=True)
        outs.append(o * lax.rsqrt(ms + EPS) * gsub_ref[...])
    o_ref[...] = jnp.concatenate(outs, axis=0).T.astype(o_ref.dtype)


def _diff_prompt_call(qb_t, kb16, vb_t, bias, lam_row, gsub_col, n, s):
    nq = s // TILE
    w = H_B * 2 * HD
    return pl.pallas_call(
        _diff_kernel,
        grid=(n, nq),
        in_specs=[
            pl.BlockSpec((None, w, TILE), lambda b, i: (b * nq + i, 0, 0)),
            pl.BlockSpec((None, s, w), lambda b, i: (b, 0, 0)),
            pl.BlockSpec((None, nq, w, TILE), lambda b, i: (b, 0, 0, 0)),
            pl.BlockSpec((H_B, 2, TILE, TILE), lambda b, i: (0, 0, 0, 0)),
            pl.BlockSpec((1, TILE), lambda b, i: (0, 0)),
            pl.BlockSpec((2 * HD, 1), lambda b, i: (0, 0)),
        ],
        out_specs=pl.BlockSpec((TILE, w), lambda b, i: (b * nq + i, 0)),
        out_shape=jax.ShapeDtypeStruct((n * s, w), BF16),
        scratch_shapes=[
            pltpu.VMEM((2 * H_B, 2 * HD, TILE), BF16),
            pltpu.VMEM((2 * H_B, TILE), F32),
            pltpu.VMEM((2 * H_B, TILE), F32),
            pltpu.VMEM((2 * H_B, 2 * HD, TILE), F32),
        ],
        compiler_params=_cparams(2), name="diff_prompt",
    )(qb_t, kb16.reshape(n, s, w), vb_t.reshape(n, nq, w, TILE), bias, lam_row, gsub_col)


def _bias_tiles(tbl):
    kk = jnp.arange(TILE, dtype=I32)[:, None]
    qq = jnp.arange(TILE, dtype=I32)[None, :]
    far = tbl[NUM_BUCKETS - 1]
    tiles = []
    for dist in (qq + TILE - kk, qq - kk):
        b = tbl[_t5_bucket(dist)] - far
        tiles.append(jnp.transpose(b, (2, 0, 1)))
    return jnp.stack(tiles, axis=1).astype(F32)


def _post_kernel(y_ref, oa_ref, ob_ref, woa_ref, wob_ref, gc_ref, wmq_ref, gq_ref, bd_ref,
                 mk_ref, mv_ref, wmo_ref, out_ref):
    y1 = y_ref[...] + jnp.dot(oa_ref[...], woa_ref[...], preferred_element_type=F32) \
        + jnp.dot(ob_ref[...], wob_ref[...], preferred_element_type=F32)
    hn = _rms_rows(y1, gc_ref[...]).astype(BF16)
    q = jnp.dot(hn, wmq_ref[...], preferred_element_type=F32)
    q2 = q * q
    q2_hi = q2.astype(BF16)
    q2_lo = (q2 - q2_hi.astype(F32)).astype(BF16)
    ms = jnp.dot(q2_hi, bd_ref[...], preferred_element_type=F32) + jnp.dot(q2_lo, bd_ref[...], preferred_element_type=F32)
    qn = (q * lax.rsqrt(ms + EPS) * gq_ref[...]).astype(BF16)
    o = jnp.zeros(q.shape, F32)
    for h in range(MEM_H):
        s = jnp.dot(qn, mk_ref[h], preferred_element_type=F32)
        p = jnp.exp(s - jnp.max(s, axis=1, keepdims=True))
        p = p / jnp.sum(p, axis=1, keepdims=True)
        o = o + jnp.dot(p.astype(BF16), mv_ref[h], preferred_element_type=F32)
    out_ref[...] = y1 + jnp.dot(o.astype(BF16), wmo_ref[...], preferred_element_type=F32)


def _post_call(y, oa, ob, w_o16, g_cross, w_mq16, g_mq, mk_pad, mv_pad, w_mo16, tm, tiles_per_batch):
    t_tot, d = y.shape
    wa = H_A * HD
    wq = MEM_H * HD
    n_mem = mk_pad.shape[-1]
    bd = jnp.kron(jnp.eye(MEM_H, dtype=F32), jnp.full((HD, HD), 1.0 / HD, F32)).astype(BF16)
    gq = (jnp.tile(g_mq.astype(F32), MEM_H) * (HD ** -0.5)).reshape(1, wq)
    const = lambda shape: pl.BlockSpec(shape, lambda j: (0,) * len(shape))
    return pl.pallas_call(
        _post_kernel,
        grid=(t_tot // tm,),
        in_specs=[
            pl.BlockSpec((tm, d), lambda j: (j, 0)),
            pl.BlockSpec((tm, wa), lambda j: (j, 0)),
            pl.BlockSpec((tm, wa), lambda j: (j, 0)),
            const((wa, d)), const((wa, d)), const((1, d)), const((d, wq)), const((1, wq)), const((wq, wq)),
            pl.BlockSpec((None, MEM_H, wq, n_mem), lambda j: (j // tiles_per_batch, 0, 0, 0)),
            pl.BlockSpec((None, MEM_H, n_mem, wq), lambda j: (j // tiles_per_batch, 0, 0, 0)),
            const((wq, d)),
        ],
        out_specs=pl.BlockSpec((tm, d), lambda j: (j, 0)),
        out_shape=jax.ShapeDtypeStruct((t_tot, d), F32),
        compiler_params=_cparams(1), name="post_mixer",
    )(y, oa, ob, w_o16[:wa], w_o16[wa:], g_cross.reshape(1, d).astype(F32), w_mq16, gq, bd,
      mk_pad, mv_pad, w_mo16)


def _pad_mem(mk, mv):
    eye = jnp.eye(MEM_H, dtype=mk.dtype)
    mk_pad = jnp.einsum("bshd,hg->bhgds", mk, eye).reshape(mk.shape[0], MEM_H, MEM_H * HD, mk.shape[1])
    mv_pad = jnp.einsum("bshd,hg->bhsgd", mv, eye).reshape(mv.shape[0], MEM_H, mv.shape[1], MEM_H * HD)
    return mk_pad.astype(BF16), mv_pad.astype(BF16)


def _ffn_kernel(x_ref, g_ref, wg_ref, wu_ref, wd_ref, out_ref, hn_ref, acc_ref):
    f = pl.program_id(1)

    @pl.when(f == 0)
    def _():
        x = x_ref[...]
        hn_ref[...] = _rms_rows(x, g_ref[...]).astype(BF16)
        acc_ref[...] = x

    hn = hn_ref[...]
    g = jnp.dot(hn, wg_ref[...], preferred_element_type=F32)
    u = jnp.dot(hn, wu_ref[...], preferred_element_type=F32)
    a = (g / (1.0 + jnp.exp(-g))) * u
    acc_ref[...] += jnp.dot(a.astype(BF16), wd_ref[...], preferred_element_type=F32)

    @pl.when(f == pl.num_programs(1) - 1)
    def _():
        out_ref[...] = acc_ref[...]


def _ffn_call(x, g_ffn, wg16, wu16, wd16, tm):
    t_tot, d = x.shape
    dff = wg16.shape[1]
    nf = 2
    tf = dff // nf
    return pl.pallas_call(
        _ffn_kernel,
        grid=(t_tot // tm, nf),
        in_specs=[
            pl.BlockSpec((tm, d), lambda j, f: (j, 0)),
            pl.BlockSpec((1, d), lambda j, f: (0, 0)),
            pl.BlockSpec((d, tf), lambda j, f: (0, f)),
            pl.BlockSpec((d, tf), lambda j, f: (0, f)),
            pl.BlockSpec((tf, d), lambda j, f: (f, 0)),
        ],
        out_specs=pl.BlockSpec((tm, d), lambda j, f: (j, 0)),
        out_shape=jax.ShapeDtypeStruct((t_tot, d), F32),
        scratch_shapes=[pltpu.VMEM((tm, d), BF16), pltpu.VMEM((tm, d), F32)],
        compiler_params=_cparams(2), name="ffn",
    )(x, g_ffn.reshape(1, d).astype(F32), wg16, wu16, wd16)


def _page_specs(block, layer, n_pages_step):
    def make(p):
        return pl.BlockSpec((None, None) + block,
                            lambda b, j, pt: (layer, pt[b, j * n_pages_step + p]) + (0,) * len(block))
    return [make(p) for p in range(n_pages_step)]


def _idx_score_kernel(pt_ref, iq_ref, iw_ref, iknew_ref, *refs, n_pages_step):
    page_refs = refs[:n_pages_step]
    sc_ref, scn_ref = refs[n_pages_step:]
    iq = iq_ref[...].astype(BF16)
    iw = iw_ref[...]

    def score(keys):
        d = lax.dot_general(iq, keys.astype(BF16), (((1,), (1,)), ((), ())), preferred_element_type=F32)
        return jnp.sum(jnp.maximum(d, 0.0) * iw, axis=0, keepdims=True)

    for p in range(n_pages_step):
        sc_ref[:, p * PAGE:(p + 1) * PAGE] = score(page_refs[p][...])

    @pl.when(pl.program_id(1) == 0)
    def _():
        lane = lax.broadcasted_iota(I32, (1, PAGE), 1)
        scn_ref[...] = jnp.where(lane == 0, score(iknew_ref[...]), NEG)


def _idx_score_call(page_table, cache_ik, layer, iq, iw_col, iknew_pad):
    bsz, n_pages = page_table.shape
    pps = PAGES_PER_STEP
    grid_spec = pltpu.PrefetchScalarGridSpec(
        num_scalar_prefetch=1, grid=(bsz, n_pages // pps),
        in_specs=[
            pl.BlockSpec((None, H_I, D_IDX), lambda b, j, pt: (b, 0, 0)),
            pl.BlockSpec((None, H_I, 1), lambda b, j, pt: (b, 0, 0)),
            pl.BlockSpec((None, PAGE, D_IDX), lambda b, j, pt: (b, 0, 0)),
        ] + _page_specs((PAGE, D_IDX), layer, pps),
        out_specs=[
            pl.BlockSpec((None, 1, pps * PAGE), lambda b, j, pt: (b, 0, j)),
            pl.BlockSpec((None, 1, PAGE), lambda b, j, pt: (b, 0, 0)),
        ])
    return pl.pallas_call(
        functools.partial(_idx_score_kernel, n_pages_step=pps),
        grid_spec=grid_spec,
        out_shape=[jax.ShapeDtypeStruct((bsz, 1, n_pages * PAGE), F32),
                   jax.ShapeDtypeStruct((bsz, 1, PAGE), F32)],
        compiler_params=_cparams(2), name="idx_score_sample",
    )(page_table, iq, iw_col, iknew_pad, *([cache_ik] * pps))


def _dsa_sample_kernel(pt_ref, scfull_ref, sc_ref, bias_ref, biasn_ref, q_ref, knew_ref, vnew_ref, *refs,
                       n_pages_step, topk, past, nbits):
    pps = n_pages_step
    k_refs = refs[:pps]
    v_refs = refs[pps:2 * pps]
    o_ref, tau_ref, jl_ref, m_ref, l_ref, acc_ref = refs[2 * pps:]
    j = pl.program_id(1)
    lane = lax.broadcasted_iota(I32, (1, PAGE), 1)

    @pl.when(j == 0)
    def _():
        row = scfull_ref[...]
        nk = row.shape[1]
        kidx = lax.broadcasted_iota(I32, (1, nk), 1)
        kf = jnp.float32(topk)

        def bit_step(it, carry):
            prefix, cge = carry
            cand = prefix | jnp.left_shift(jnp.int32(1), 31 - it)
            thr = _key_to_f32(cand ^ jnp.int32(INT_MIN))[:, 0:1]
            cnt = jnp.sum(jnp.where(row >= thr, 1.0, 0.0), axis=1, keepdims=True)
            take = cnt >= kf
            return jnp.where(take, cand, prefix), jnp.where(take, cnt, cge)

        prefix, cge = lax.fori_loop(0, 32, bit_step, (jnp.zeros((1, PAGE), I32), jnp.zeros((1, PAGE), F32)))
        tau = _key_to_f32(prefix ^ jnp.int32(INT_MIN))
        tau1 = tau[:, 0:1]
        need = kf - jnp.sum(jnp.where(row > tau1, 1.0, 0.0), axis=1, keepdims=True)

        def idx_step(it, jlim):
            cand = jlim | jnp.left_shift(jnp.int32(1), nbits - 1 - it)
            cnt = jnp.sum(jnp.where((row == tau1) & (kidx < cand[:, 0:1]), 1.0, 0.0), axis=1, keepdims=True)
            return jnp.where(cnt < need, cand, jlim)

        jlim = lax.fori_loop(0, nbits, idx_step, jnp.zeros((1, PAGE), I32))
        tau_ref[...] = tau
        jl_ref[...] = jnp.where(cge > kf, jlim, 2 ** 30)
        m_ref[...] = jnp.full(m_ref.shape, NEG, F32)
        l_ref[...] = jnp.zeros(l_ref.shape, F32)
        acc_ref[...] = jnp.zeros(acc_ref.shape, F32)

    tau = tau_ref[...]
    jlim = jl_ref[...]
    q = q_ref[...].astype(BF16)

    def attend(keys, vals, sc, bias, key0):
        sel = (sc > tau) | ((sc == tau) & ((key0 + lane) <= jlim))
        s = lax.dot_general(q, keys.astype(BF16), (((1,), (1,)), ((), ())), preferred_element_type=F32)
        s = jnp.where(sel, s + bias, NEG)
        m_old = m_ref[...]
        m_new = jnp.maximum(m_old, jnp.max(s, axis=1, keepdims=True))
        alpha = jnp.exp(m_old - m_new)
        p = jnp.exp(s - m_new)
        l_ref[...] = alpha * l_ref[...] + jnp.sum(p, axis=1, keepdims=True)
        m_ref[...] = m_new
        acc_ref[...] = acc_ref[...] * alpha + jnp.dot(p.astype(BF16), vals.astype(BF16), preferred_element_type=F32)

    for p in range(pps):
        attend(k_refs[p][...], v_refs[p][...], sc_ref[:, p * PAGE:(p + 1) * PAGE],
               bias_ref[:, p * PAGE:(p + 1) * PAGE], (j * pps + p) * PAGE)

    @pl.when(j == pl.num_programs(1) - 1)
    def _():
        attend(knew_ref[...], vnew_ref[...], scfull_ref[:, past:past + PAGE], biasn_ref[...], past)
        o_ref[...] = acc_ref[...] / l_ref[...]


def _dsa_sample_call(page_table, cache_k, cache_v, layer, sc_all, bias_all, qpad, knew_pad, vnew_pad, topk):
    bsz, n_pages = page_table.shape
    pps = PAGES_PER_STEP
    past = n_pages * PAGE
    nk = past + PAGE
    w = KV_A * HD
    nbits = max(1, int(math.ceil(math.log2(nk))))
    grid_spec = pltpu.PrefetchScalarGridSpec(
        num_scalar_prefetch=1, grid=(bsz, n_pages // pps),
        in_specs=[
            pl.BlockSpec((None, 1, nk), lambda b, j, pt: (b, 0, 0)),
            pl.BlockSpec((None, 1, pps * PAGE), lambda b, j, pt: (b, 0, j)),
            pl.BlockSpec((H_A, pps * PAGE), lambda b, j, pt: (0, j)),
            pl.BlockSpec((H_A, PAGE), lambda b, j, pt: (0, n_pages)),
            pl.BlockSpec((None, H_A, w), lambda b, j, pt: (b, 0, 0)),
            pl.BlockSpec((None, PAGE, w), lambda b, j, pt: (b, 0, 0)),
            pl.BlockSpec((None, PAGE, w), lambda b, j, pt: (b, 0, 0)),
        ] + _page_specs((PAGE, w), layer, pps) + _page_specs((PAGE, w), layer, pps),
        out_specs=pl.BlockSpec((None, H_A, w), lambda b, j, pt: (b, 0, 0)),
        scratch_shapes=[pltpu.VMEM((1, PAGE), F32), pltpu.VMEM((1, PAGE), I32),
                        pltpu.VMEM((H_A, 1), F32), pltpu.VMEM((H_A, 1), F32), pltpu.VMEM((H_A, w), F32)])
    return pl.pallas_call(
        functools.partial(_dsa_sample_kernel, n_pages_step=pps, topk=topk, past=past, nbits=nbits),
        grid_spec=grid_spec,
        out_shape=jax.ShapeDtypeStruct((bsz, H_A, w), F32),
        compiler_params=_cparams(2), name="dsa_sample",
    )(page_table, sc_all, sc_all, bias_all, bias_all, qpad, knew_pad, vnew_pad,
      *([cache_k] * pps), *([cache_v] * pps))


def _diff_sample_kernel(pt_ref, bias_ref, biasn_ref, q_ref, knew_ref, vnew_ref, lam_ref, gsub_ref, *refs,
                        n_pages_step):
    pps = n_pages_step
    k_refs = refs[:pps]
    v_refs = refs[pps:2 * pps]
    o_ref, m_ref, l_ref, acc_ref = refs[2 * pps:]
    j = pl.program_id(1)
    lane = lax.broadcasted_iota(I32, (1, PAGE), 1)

    @pl.when(j == 0)
    def _():
        m_ref[...] = jnp.full(m_ref.shape, NEG, F32)
        l_ref[...] = jnp.zeros(l_ref.shape, F32)
        acc_ref[...] = jnp.zeros(acc_ref.shape, F32)

    q = q_ref[...].astype(BF16)

    def attend(keys, vals, bias, valid):
        s = lax.dot_general(q, keys.astype(BF16), (((1,), (1,)), ((), ())), preferred_element_type=F32) + bias
        if valid is not None:
            s = jnp.where(valid, s, NEG)
        m_old = m_ref[...]
        m_new = jnp.maximum(m_old, jnp.max(s, axis=1, keepdims=True))
        alpha = jnp.exp(m_old - m_new)
        p = jnp.exp(s - m_new)
        l_ref[...] = alpha * l_ref[...] + jnp.sum(p, axis=1, keepdims=True)
        m_ref[...] = m_new
        acc_ref[...] = acc_ref[...] * alpha + jnp.dot(p.astype(BF16), vals.astype(BF16), preferred_element_type=F32)

    for p in range(pps):
        attend(k_refs[p][...], v_refs[p][...], bias_ref[:, p * PAGE:(p + 1) * PAGE], None)

    @pl.when(j == pl.num_programs(1) - 1)
    def _():
        attend(knew_ref[...], vnew_ref[...], biasn_ref[...], lane == 0)
        o = acc_ref[...] / l_ref[...]
        lam = lam_ref[...]
        outs = []
        for h in range(H_B):
            sl = slice(h * 2 * HD, (h + 1) * 2 * HD)
            oh = o[2 * h:2 * h + 1, sl] - lam * o[2 * h + 1:2 * h + 2, sl]
            ms = jnp.mean(oh * oh, axis=1, keepdims=True)
            outs.append(oh * lax.rsqrt(ms + EPS) * gsub_ref[...])
        o_ref[...] = jnp.concatenate(outs, axis=1)


def _diff_sample_call(page_table, cache_k, cache_v, layer, bias_all, qbd, knew_pad, vnew_pad, lam_row, gsub_row):
    bsz, n_pages = page_table.shape
    pps = PAGES_PER_STEP
    w = H_B * 2 * HD
    nm = 2 * H_B
    grid_spec = pltpu.PrefetchScalarGridSpec(
        num_scalar_prefetch=1, grid=(bsz, n_pages // pps),
        in_specs=[
            pl.BlockSpec((nm, pps * PAGE), lambda b, j, pt: (0, j)),
            pl.BlockSpec((nm, PAGE), lambda b, j, pt: (0, n_pages)),
            pl.BlockSpec((None, nm, w), lambda b, j, pt: (b, 0, 0)),
            pl.BlockSpec((None, PAGE, w), lambda b, j, pt: (b, 0, 0)),
            pl.BlockSpec((None, PAGE, w), lambda b, j, pt: (b, 0, 0)),
            pl.BlockSpec((1, 2 * HD), lambda b, j, pt: (0, 0)),
            pl.BlockSpec((1, 2 * HD), lambda b, j, pt: (0, 0)),
        ] + _page_specs((PAGE, w), layer, pps) + _page_specs((PAGE, w), layer, pps),
        out_specs=pl.BlockSpec((None, 1, w), lambda b, j, pt: (b, 0, 0)),
        scratch_shapes=[pltpu.VMEM((nm, 1), F32), pltpu.VMEM((nm, 1), F32), pltpu.VMEM((nm, w), F32)])
    return pl.pallas_call(
        functools.partial(_diff_sample_kernel, n_pages_step=pps),
        grid_spec=grid_spec,
        out_shape=jax.ShapeDtypeStruct((bsz, 1, w), F32),
        compiler_params=_cparams(2), name="diff_sample",
    )(page_table, bias_all, bias_all, qbd, knew_pad, vnew_pad, lam_row, gsub_row,
      *([cache_k] * pps), *([cache_v] * pps))


def _first_row_page(x):
    return jnp.pad(x[:, None, :], ((0, 0), (0, PAGE - 1), (0, 0)))


def kernel(x_prompt, x_sample, cache_a_k, cache_a_v, cache_a_ik, cache_b_k, cache_b_v, cache_mem_k, cache_mem_v, page_table, mem_prompt, rel_bias, g_mix, w_in, g_qa, g_ka, g_qb, g_kb, lam, g_sub, w_o, g_cross, g_mem, w_mq, w_mkv, g_mq, g_mk, w_mo, g_ffn, w_gu, w_down):
    n, s, d = x_prompt.shape
    bsz = x_sample.shape[0]
    depth = w_in.shape[0]
    n_mem = mem_prompt.shape[1]
    n_pages = page_table.shape[1]
    past = n_pages * PAGE
    n_pool = cache_a_k.shape[1]
    dff = w_down.shape[1]
    assert x_sample.shape[1] == 1 and s % TILE == 0 and n_pages % PAGES_PER_STEP == 0

    tbl_a = rel_bias[:, :H_A].astype(F32)
    tbl_b = rel_bias[:, H_A:].astype(F32)
    bias_a = _bias_tiles(tbl_a)
    bias_b = _bias_tiles(tbl_b)
    kpos = jnp.arange(past + PAGE, dtype=I32)
    bkt = _t5_bucket(past - kpos)
    bias_sa = tbl_a[bkt].T
    bias_sb = jnp.repeat(tbl_b[bkt].T, 2, axis=0)

    ck_a = cache_a_k.reshape(depth, n_pool, PAGE, KV_A * HD)
    cv_a = cache_a_v.reshape(depth, n_pool, PAGE, KV_A * HD)
    ck_b = cache_b_k.reshape(depth, n_pool, PAGE, H_B * 2 * HD)
    cv_b = cache_b_v.reshape(depth, n_pool, PAGE, H_B * 2 * HD)

    srows = bsz * SAMPLE_ROWS
    srows_pad = -(-srows // (2 * TILE)) * (2 * TILE)
    ys = jnp.zeros((srows_pad, d), F32).at[0:srows:SAMPLE_ROWS].set(x_sample[:, 0, :])
    yp = x_prompt.reshape(n * s, d)
    mem2d = mem_prompt.reshape(n * n_mem, d)
    topk_s = min(TOPK_MAX, (past + 1) // 4)

    rows_p = [[] for _ in range(7)]
    rows_s = [[] for _ in range(5)]
    for l in range(depth):
        lam_init = 0.8 - 0.6 * math.exp(-0.3 * l)
        lf = lam[l].astype(F32)
        lam_l = jnp.exp(jnp.sum(lf[0] * lf[1])) - jnp.exp(jnp.sum(lf[2] * lf[3])) + lam_init
        gsub = g_sub[l].astype(F32) * (1.0 - lam_init)
        w_t, gcol = _in_proj_weights(w_in[l], g_qa[l], g_ka[l], g_qb[l], g_kb[l])
        w_o16 = w_o[l].astype(BF16)
        w_mq16 = w_mq[l].astype(BF16)
        w_mo16 = w_mo[l].astype(BF16)
        wg16 = w_gu[l, :, :dff].astype(BF16)
        wu16 = w_gu[l, :, dff:].astype(BF16)
        wd16 = w_down[l].astype(BF16)

        wkv_t = w_mkv[l].T
        wkv_hi = wkv_t.astype(BF16)
        wkv_lo = (wkv_t - wkv_hi.astype(F32)).astype(BF16)
        wq = MEM_H * HD
        gcol_m = jnp.concatenate([jnp.tile(g_mk[l].astype(F32), MEM_H), jnp.ones((wq,), F32)]).reshape(-1, 1)
        mem_segs = ((0, wq, True, (("R32", 0, wq),)), (wq, wq, False, (("R32", 0, wq),)))
        mk, mv = _proj_call(mem2d, g_mem[l], wkv_hi, gcol_m, mem_segs, TILE, w_lo=wkv_lo)
        mk = mk.reshape(n, n_mem, MEM_H, HD)
        mv = mv.reshape(n, n_mem, MEM_H, HD)
        mk_pad, mv_pad = _pad_mem(mk, mv)

        (qa_t, iq_t, qb_t, ka, ka16, kb, kb16, va, va_t, vb, vb_t, ik, ik16, iw_t) = _proj_call(
            yp, g_mix[l], w_t, gcol, _IN_SEGS, TILE)
        o_a = _dsa_prompt_call(qa_t, iq_t, iw_t, ik16, ka16, va_t, bias_a, n, s)
        o_b = _diff_prompt_call(qb_t, kb16, vb_t, bias_b, jnp.full((1, TILE), lam_l, F32),
                                gsub.reshape(-1, 1), n, s)
        tm_post = min(512, s)
        yp = _post_call(yp, o_a, o_b, w_o16, g_cross[l], w_mq16, g_mq[l], mk_pad, mv_pad, w_mo16,
                        tm_post, s // tm_post)
        yp = _ffn_call(yp, g_ffn[l], wg16, wu16, wd16, min(512, n * s))
        for i, r in enumerate((ka.reshape(n, s, KV_A, HD), va.reshape(n, s, KV_A, HD), ik.reshape(n, s, D_IDX),
                               kb.reshape(n, s, H_B, 2 * HD), vb.reshape(n, s, H_B, 2 * HD), mk, mv)):
            rows_p[i].append(r)

        (sqa_t, siq_t, sqb_t, ska, _, skb, _, sva, _, svb, _, sik, _, siw_t) = _proj_call(
            ys, g_mix[l], w_t, gcol, _IN_SEGS, TILE)

        def rows_of_t(a_t, width):
            a = jnp.transpose(a_t, (0, 2, 1)).reshape(-1, a_t.shape[1])
            return a[0:srows:SAMPLE_ROWS, :width].astype(F32)

        take = lambda a: a[0:srows:SAMPLE_ROWS]
        s_qa = rows_of_t(sqa_t, H_A * HD).reshape(bsz, H_A, HD)
        s_iq = rows_of_t(siq_t, H_I * D_IDX).reshape(bsz, H_I, D_IDX)
        s_qb = rows_of_t(sqb_t, 2 * H_B * HD).reshape(bsz, 2 * H_B, HD)
        s_iw = rows_of_t(siw_t, H_I)
        s_ka, s_va, s_ik, s_kb, s_vb = take(ska), take(sva), take(sik), take(skb), take(svb)

        sc_past, sc_new = _idx_score_call(page_table, cache_a_ik, l, s_iq, s_iw.reshape(bsz, H_I, 1),
                                          _first_row_page(s_ik))
        sc_all = jnp.concatenate([sc_past, sc_new], axis=2)
        grp = jnp.eye(KV_A, dtype=F32)[jnp.arange(H_A) // G_A]
        qpad = (s_qa[:, :, None, :] * grp[None, :, :, None]).reshape(bsz, H_A, KV_A * HD)
        o_sa = _dsa_sample_call(page_table, ck_a, cv_a, l, sc_all, bias_sa, qpad,
                                _first_row_page(s_ka), _first_row_page(s_va), topk_s)
        o_sa = jnp.einsum("bhgd,hg->bhd", o_sa.reshape(bsz, H_A, KV_A, HD), grp).reshape(bsz, H_A * HD)
        mpe = jnp.eye(2 * H_B, dtype=F32)
        qbd = (s_qb[:, :, None, :] * mpe[None, :, :, None]).reshape(bsz, 2 * H_B, 2 * H_B * HD)
        o_sb = _diff_sample_call(page_table, ck_b, cv_b, l, bias_sb, qbd, _first_row_page(s_kb),
                                 _first_row_page(s_vb), jnp.full((1, 2 * HD), lam_l, F32),
                                 gsub.reshape(1, -1)).reshape(bsz, H_B * 2 * HD)

        def scatter_rows(o):
            return jnp.zeros((srows_pad, o.shape[1]), BF16).at[0:srows:SAMPLE_ROWS].set(o.astype(BF16))

        smk_pad, smv_pad = _pad_mem(cache_mem_k[l], cache_mem_v[l])
        n_fill = srows_pad // SAMPLE_ROWS - bsz
        if n_fill:
            smk_pad = jnp.pad(smk_pad, ((0, n_fill), (0, 0), (0, 0), (0, 0)))
            smv_pad = jnp.pad(smv_pad, ((0, n_fill), (0, 0), (0, 0), (0, 0)))
        ys = _post_call(ys, scatter_rows(o_sa), scatter_rows(o_sb), w_o16, g_cross[l], w_mq16, g_mq[l],
                        smk_pad, smv_pad, w_mo16, SAMPLE_ROWS, 1)
        ys = _ffn_call(ys, g_ffn[l], wg16, wu16, wd16, min(512, srows_pad))
        for i, r in enumerate((s_ka.reshape(bsz, 1, KV_A, HD), s_va.reshape(bsz, 1, KV_A, HD),
                               s_ik.reshape(bsz, 1, D_IDX), s_kb.reshape(bsz, 1, H_B, 2 * HD),
                               s_vb.reshape(bsz, 1, H_B, 2 * HD))):
            rows_s[i].append(r)

    y_sample = ys[0:srows:SAMPLE_ROWS].reshape(bsz, 1, d)
    return (yp.reshape(n, s, d), y_sample,
            jnp.stack(rows_p[0]), jnp.stack(rows_p[1]), jnp.stack(rows_p[2]), jnp.stack(rows_p[3]),
            jnp.stack(rows_p[4]), jnp.stack(rows_p[5]), jnp.stack(rows_p[6]),
            jnp.stack(rows_s[0]), jnp.stack(rows_s[1]), jnp.stack(rows_s[2]), jnp.stack(rows_s[3]),
            jnp.stack(rows_s[4]))
```

```python
import functools
import math

import numpy as np
import jax
import jax.numpy as jnp
from jax import lax
from jax.experimental import pallas as pl
from jax.experimental.pallas import tpu as pltpu

F32 = jnp.float32
BF16 = jnp.bfloat16
I32 = jnp.int32
I16 = jnp.int16

HD = 64
H_A = 8
KV_A = 2
G_A = H_A // KV_A
H_I = 8
D_IDX = 64
H_B = 4
MEM_H = 4
TOPK_MAX = 256
PAGE = 128
NUM_BUCKETS = 32
MAX_EXACT = 16
MAX_DIST = 128
EPS = 1e-6
NEG = -1e30

TILE = 256
SAMPLE_ROWS = 16
PAGES_PER_STEP = 16
LOG2E = 1.4426950408889634
BOUND_LOG2_MAX = 60.0
ONES_ROWS = 16
MASK_BIAS = -30000.0
V7X_VMEM_LIMIT = 56 * 1024 * 1024
INT_MIN = -2 ** 31
I16_MAX = 2 ** 15 - 1
I16_MIN = -2 ** 15


def _cparams(n_axes):
    return pltpu.CompilerParams(dimension_semantics=("arbitrary",) * n_axes,
                                vmem_limit_bytes=V7X_VMEM_LIMIT)


def _t5_bucket(dist):
    n = jnp.maximum(dist, 0)
    nf = jnp.maximum(n, 1).astype(F32)
    large = MAX_EXACT + (jnp.log(nf / MAX_EXACT) / math.log(MAX_DIST / MAX_EXACT)
                         * (NUM_BUCKETS - MAX_EXACT)).astype(I32)
    large = jnp.minimum(large, NUM_BUCKETS - 1)
    return jnp.where(n < MAX_EXACT, n, large)


def _bias_lookup(tbl, dist):
    onehot = jax.nn.one_hot(_t5_bucket(dist), NUM_BUCKETS, dtype=F32)
    return jnp.einsum("...b,bh->...h", onehot, tbl, precision=lax.Precision.HIGHEST)


def _rms_rows(x, g_row):
    ms = jnp.mean(x * x, axis=1, keepdims=True)
    return x * lax.rsqrt(ms + EPS) * g_row


def _proj_kernel(*refs, segs, hi_prec, tm):
    if hi_prec:
        x_ref, g_ref, w_ref, wlo_ref, gcol_ref = refs[:5]
        out_refs = refs[5:]
    else:
        x_ref, g_ref, w_ref, gcol_ref = refs[:4]
        wlo_ref = None
        out_refs = refs[4:]
    xn = _rms_rows(x_ref[...], g_ref[...])
    xt = xn.T
    xt_hi = xt.astype(BF16)
    if hi_prec:
        xt_lo = (xt - xt_hi.astype(F32)).astype(BF16)
    oi = 0
    for (r0, nr, norm, outs) in segs:
        p = jnp.dot(w_ref[r0:r0 + nr, :], xt_hi, preferred_element_type=F32)
        if hi_prec:
            p = p + jnp.dot(wlo_ref[r0:r0 + nr, :], xt_hi, preferred_element_type=F32)
            p = p + jnp.dot(w_ref[r0:r0 + nr, :], xt_lo, preferred_element_type=F32)
        if norm:
            p3 = p.reshape(nr // HD, HD, tm)
            ss = jnp.sum(p3 * p3, axis=1, keepdims=True)
            p = (p3 * lax.rsqrt(ss * (1.0 / HD) + EPS)).reshape(nr, tm)
        p = p * gcol_ref[r0:r0 + nr, :]
        pt = None
        for (kind, off, n) in outs:
            o_ref = out_refs[oi]
            oi += 1
            if kind[0] == "T":
                o_ref[...] = p[off:off + n, :].astype(o_ref.dtype)
            else:
                if pt is None:
                    pt = p.T
                o_ref[...] = pt[:, off:off + n].astype(o_ref.dtype)


def _proj_call(x2d, g, w_t, gcol, segs, tm, w_lo=None):
    t_tot, d = x2d.shape
    r_tot = w_t.shape[0]
    nt = t_tot // tm
    hi_prec = w_lo is not None
    in_specs = [pl.BlockSpec((tm, d), lambda j: (j, 0)),
                pl.BlockSpec((1, d), lambda j: (0, 0)),
                pl.BlockSpec((r_tot, d), lambda j: (0, 0))]
    args = [x2d, g.reshape(1, d).astype(F32), w_t]
    if hi_prec:
        in_specs.append(pl.BlockSpec((r_tot, d), lambda j: (0, 0)))
        args.append(w_lo)
    in_specs.append(pl.BlockSpec((r_tot, 1), lambda j: (0, 0)))
    args.append(gcol)
    out_shapes, out_specs = [], []
    for (_, _, _, outs) in segs:
        for (kind, _, n) in outs:
            dt = BF16 if kind.endswith("16") else F32
            if kind[0] == "T":
                out_shapes.append(jax.ShapeDtypeStruct((nt, n, tm), dt))
                out_specs.append(pl.BlockSpec((None, n, tm), lambda j: (j, 0, 0)))
            else:
                out_shapes.append(jax.ShapeDtypeStruct((t_tot, n), dt))
                out_specs.append(pl.BlockSpec((tm, n), lambda j: (j, 0)))
    return pl.pallas_call(
        functools.partial(_proj_kernel, segs=tuple(segs), hi_prec=hi_prec, tm=tm),
        grid=(nt,), in_specs=in_specs, out_specs=out_specs, out_shape=out_shapes,
        compiler_params=_cparams(1), name="proj")(*args)


_IN_SEGS = (
    (0, 512, True, (("T16", 0, 512),)),
    (512, 512, False, (("T16", 0, 512),)),
    (1024, 512, True, (("T16", 0, 512),)),
    (1536, 128, True, (("R32", 0, 128), ("R16", 0, 128))),
    (1664, 512, True, (("R32", 0, 512), ("R16", 0, 512))),
    (2176, 128, False, (("R32", 0, 128), ("T16", 0, 128))),
    (2304, 512, False, (("R32", 0, 512), ("T16", 0, 512))),
    (2816, 128, False, (("R32", 0, 64), ("R16", 0, 64), ("T32", 64, 16))),
)


def _in_proj_weights(w_in_l, g_qa_l, g_ka_l, g_qb_l, g_kb_l):
    d = w_in_l.shape[0]
    widths = (H_A * HD, KV_A * HD, KV_A * HD, H_I * D_IDX, D_IDX, H_I, H_B * 2 * HD, H_B * 2 * HD, H_B * 2 * HD)
    offs = np.cumsum((0,) + widths)
    qa, ka, va, iq, ik, iw, qb, kb, vb = [w_in_l[:, offs[i]:offs[i + 1]] for i in range(9)]
    pad = jnp.zeros((d, 128 - D_IDX - H_I), w_in_l.dtype)
    w = jnp.concatenate([qa, iq, qb, ka, kb, va, vb, ik, iw, pad], axis=1)
    one = lambda n: jnp.ones((n,), F32)
    idx_scale = (H_I ** -0.5) * (D_IDX ** -0.5)
    gcol = jnp.concatenate([
        jnp.tile(g_qa_l.astype(F32), H_A) * (HD ** -0.5 * LOG2E), one(512),
        jnp.tile(g_qb_l.astype(F32), 2 * H_B) * (HD ** -0.5 * LOG2E),
        jnp.tile(g_ka_l.astype(F32), KV_A), jnp.tile(g_kb_l.astype(F32), 2 * H_B),
        one(128), one(512), one(D_IDX), one(H_I) * idx_scale, jnp.zeros((128 - D_IDX - H_I,), F32)])
    return w.T.astype(BF16), gcol.reshape(-1, 1)


def _logit_bound_ok(g_q, g_k, tbl):
    bound = (HD ** 0.5) * jnp.max(jnp.abs(g_q)) * jnp.max(jnp.abs(g_k)) + 2.0 * jnp.max(jnp.abs(tbl))
    return bound * LOG2E <= BOUND_LOG2_MAX


def _f32_to_key(x):
    bits = pltpu.bitcast(jnp.where(x == 0.0, 0.0, x), I32)
    return jnp.where(bits >= 0, bits, bits ^ jnp.int32(0x7FFFFFFF))


def _dsa_kernel(qa_ref, iq_ref, iw_ref, ik_ref, ka_ref, va_ref, bias_ref, o_ref,
                key_ref, h16_ref, qpad_ref, m_ref, l_ref, acc_ref, sel_ref, s_ref, *, topk, nbits, bounded):
    tq = TILE
    tk = TILE
    i = pl.program_id(1)
    nch = i + 1
    rows = lax.broadcasted_iota(I32, (tk, tq), 0)
    cols = lax.broadcasted_iota(I32, (tk, tq), 1)

    def chunk_rows(c):
        return pl.ds(pl.multiple_of(c * tk, tk), tk)

    iw = iw_ref[...]

    def score_chunk(c):
        kc = ik_ref[chunk_rows(c), :]
        acc = jnp.zeros((tk, tq), F32)
        for h in range(H_I):
            d = jnp.dot(kc, iq_ref[h * D_IDX:(h + 1) * D_IDX, :], preferred_element_type=F32)
            acc = acc + jnp.maximum(d, 0.0) * iw[h:h + 1, :]
        return acc

    def store_keys(c, sc):
        key = _f32_to_key(sc)
        key_ref[chunk_rows(c), :] = key
        h16_ref[chunk_rows(c), :] = (key >> 16).astype(I16)

    def far_scores(c, carry):
        store_keys(c, score_chunk(c))
        return carry

    lax.fori_loop(0, i, far_scores, 0)
    store_keys(i, jnp.where(rows <= cols, score_chunk(i), NEG))

    qpos = i * tq + lax.broadcasted_iota(I32, (1, tq), 1)
    kcnt = jnp.minimum(qpos + 1, topk)

    def count16(thr16):
        def body(c, acc):
            m = jnp.where(h16_ref[chunk_rows(c), :] >= thr16, jnp.int16(1), jnp.int16(0))
            part = m[0:16]
            for j in range(1, tk // 16):
                part = part + m[j * 16:(j + 1) * 16]
            return acc + part
        acc = lax.fori_loop(0, nch, body, jnp.zeros((16, tq), I16))
        return jnp.sum(acc.astype(I32), axis=0, keepdims=True)

    def bisect16():
        def step(it, carry):
            prefix, cge = carry
            cand = prefix | jnp.left_shift(jnp.int32(1), 15 - it)
            cnt = count16((cand + I16_MIN).astype(I16))
            take = cnt >= kcnt
            return jnp.where(take, cand, prefix), jnp.where(take, cnt, cge)
        prefix, cge = lax.fori_loop(0, 16, step, (jnp.zeros((1, tq), I32), jnp.zeros((1, tq), I32)))
        return prefix + I16_MIN, cge

    hi, cge_hi = bisect16()

    def low_keys(c, carry):
        key = key_ref[chunk_rows(c), :]
        khi = key >> 16
        low = (key & 0xFFFF) + I16_MIN
        h16_ref[chunk_rows(c), :] = jnp.where(khi > hi, I16_MAX, jnp.where(khi < hi, I16_MIN, low)).astype(I16)
        return carry

    lax.fori_loop(0, nch, low_keys, 0)
    lo, cge_lo = bisect16()
    tau = hi * 65536 + (lo - I16_MIN)
    cge = jnp.where(lo == I16_MIN, cge_hi, cge_lo)

    def count32(pred):
        def body(c, acc):
            m = jnp.where(pred(key_ref[chunk_rows(c), :], c), 1.0, 0.0)
            return acc + jnp.sum(m.reshape(tk // 8, 8, tq), axis=0)
        acc = lax.fori_loop(0, nch, body, jnp.zeros((8, tq), F32))
        return jnp.sum(acc, axis=0, keepdims=True).astype(I32)

    sel_ref[...] = jnp.full((1, tq), 2 ** 30, I32)
    any_excess = jnp.max(jnp.where(cge > kcnt, 1.0, 0.0)) > 0.0

    @pl.when(any_excess)
    def _():
        need = kcnt - count32(lambda key, c: key > tau)

        def idx_step(it, jlim):
            cand = jlim | jnp.left_shift(jnp.int32(1), nbits - 1 - it)
            cnt = count32(lambda key, c: (key == tau) & ((c * tk + rows) < cand))
            return jnp.where(cnt < need, cand, jlim)

        sel_ref[...] = lax.fori_loop(0, nbits, idx_step, jnp.zeros((1, tq), I32))

    jlim = sel_ref[...]

    qpad_ref[...] = jnp.zeros(qpad_ref.shape, BF16)
    for h in range(H_A):
        g = h // G_A
        qpad_ref[h, g * HD:(g + 1) * HD, :] = qa_ref[h * HD:(h + 1) * HD, :]
    acc_ref[...] = jnp.zeros(acc_ref.shape, F32)
    if not bounded:
        m_ref[...] = jnp.full(m_ref.shape, NEG, F32)
        l_ref[...] = jnp.zeros(l_ref.shape, F32)
    ones = jnp.ones((ONES_ROWS, tk), BF16)

    def select_mask(c):
        key = key_ref[chunk_rows(c), :]
        return (key > tau) | ((key == tau) & ((c * tk + rows) <= jlim))

    def bias_kind(c):
        return jnp.clip(c - (i - 2), 0, 2)

    if bounded:
        def logits(c, slot):
            kc = ka_ref[chunk_rows(c), :]
            for h in range(H_A):
                s_ref[slot, h] = jnp.dot(kc, qpad_ref[h], preferred_element_type=F32)

        def weigh(c, slot):
            sel16 = jnp.where(select_mask(c), 1.0, 0.0).astype(BF16)
            kind = bias_kind(c)
            vt = va_ref[c]
            vext = [jnp.concatenate([vt[g * HD:(g + 1) * HD, :], ones], axis=0) for g in range(KV_A)]
            for h in range(H_A):
                p = jnp.exp2(s_ref[slot, h] + bias_ref[h, kind]).astype(BF16) * sel16
                acc_ref[h] += jnp.dot(vext[h // G_A], p, preferred_element_type=F32)

        logits(0, 0)

        def pipelined(c, carry):
            slot = lax.rem(c, 2)
            weigh(c, slot)
            logits(jnp.minimum(c + 1, i), 1 - slot)
            return carry

        lax.fori_loop(0, nch, pipelined, 0)
    else:
        def attend(c, carry):
            sel = select_mask(c)
            kind = bias_kind(c)
            kc = ka_ref[chunk_rows(c), :]
            vt = va_ref[c]
            for h in range(H_A):
                g = h // G_A
                s = jnp.dot(kc, qpad_ref[h], preferred_element_type=F32) + bias_ref[h, kind]
                s = jnp.where(sel, s, NEG)
                m_old = m_ref[h:h + 1, :]
                m_new = jnp.maximum(m_old, jnp.max(s, axis=0, keepdims=True))
                alpha = jnp.exp2(m_old - m_new)
                p = jnp.exp2(s - m_new)
                l_ref[h:h + 1, :] = alpha * l_ref[h:h + 1, :] + jnp.sum(p, axis=0, keepdims=True)
                m_ref[h:h + 1, :] = m_new
                pv = jnp.dot(vt[g * HD:(g + 1) * HD, :], p.astype(BF16), preferred_element_type=F32)
                acc_ref[h, 0:HD, :] = acc_ref[h, 0:HD, :] * alpha + pv
            return carry

        lax.fori_loop(0, nch, attend, 0)

    outs = []
    for h in range(H_A):
        den = acc_ref[h, HD:HD + 1, :] if bounded else l_ref[h:h + 1, :]
        outs.append(acc_ref[h, 0:HD, :] / den)
    o_ref[...] = jnp.concatenate(outs, axis=0).T.astype(o_ref.dtype)


def _dsa_prompt_call(qa_t, iq_t, iw_t, ik16, ka16, va_t, bias, n, s, bounded):
    nq = s // TILE
    topk = min(TOPK_MAX, s // 4)
    nbits = max(1, int(math.ceil(math.log2(s))))
    return pl.pallas_call(
        functools.partial(_dsa_kernel, topk=topk, nbits=nbits, bounded=bounded),
        grid=(n, nq),
        in_specs=[
            pl.BlockSpec((None, H_A * HD, TILE), lambda b, i: (b * nq + i, 0, 0)),
            pl.BlockSpec((None, H_I * D_IDX, TILE), lambda b, i: (b * nq + i, 0, 0)),
            pl.BlockSpec((None, 16, TILE), lambda b, i: (b * nq + i, 0, 0)),
            pl.BlockSpec((None, s, D_IDX), lambda b, i: (b, 0, 0)),
            pl.BlockSpec((None, s, KV_A * HD), lambda b, i: (b, 0, 0)),
            pl.BlockSpec((None, nq, KV_A * HD, TILE), lambda b, i: (b, 0, 0, 0)),
            pl.BlockSpec((H_A, 3, TILE, TILE), lambda b, i: (0, 0, 0, 0), pipeline_mode=pl.Buffered(1)),
        ],
        out_specs=pl.BlockSpec((TILE, H_A * HD), lambda b, i: (b * nq + i, 0)),
        out_shape=jax.ShapeDtypeStruct((n * s, H_A * HD), BF16),
        scratch_shapes=[
            pltpu.VMEM((s, TILE), I32),
            pltpu.VMEM((s, TILE), I16),
            pltpu.VMEM((H_A, KV_A * HD, TILE), BF16),
            pltpu.VMEM((H_A, TILE), F32),
            pltpu.VMEM((H_A, TILE), F32),
            pltpu.VMEM((H_A, HD + ONES_ROWS, TILE), F32),
            pltpu.VMEM((1, TILE), I32),
            pltpu.VMEM((2, H_A, TILE, TILE) if bounded else (1, 1, 8, 128), F32),
        ],
        compiler_params=_cparams(2), name="dsa_prompt_bounded" if bounded else "dsa_prompt_online",
    )(qa_t, iq_t, iw_t, ik16.reshape(n, s, D_IDX), ka16.reshape(n, s, KV_A * HD),
      va_t.reshape(n, nq, KV_A * HD, TILE), bias)


def _diff_kernel(qb_ref, kb_ref, vb_ref, bias_ref, lam_ref, gsub_ref, o_ref,
                 qpad_ref, m_ref, l_ref, acc_ref, s_ref, *, bounded):
    tq = TILE
    tk = TILE
    i = pl.program_id(1)
    nch = i + 1
    nmap = 2 * H_B
    hw = 2 * HD

    qpad_ref[...] = jnp.zeros(qpad_ref.shape, BF16)
    for mp in range(nmap):
        c = mp % 2
        qpad_ref[mp, c * HD:(c + 1) * HD, :] = qb_ref[mp * HD:(mp + 1) * HD, :]
    acc_ref[...] = jnp.zeros(acc_ref.shape, F32)
    if not bounded:
        m_ref[...] = jnp.full(m_ref.shape, NEG, F32)
        l_ref[...] = jnp.zeros(l_ref.shape, F32)
    ones = jnp.ones((ONES_ROWS, tk), BF16)

    def chunk_rows(c):
        return pl.ds(pl.multiple_of(c * tk, tk), tk)

    def bias_kind(c):
        return jnp.clip(c - (i - 2), 0, 2)

    if bounded:
        def logits(c, slot):
            for h in range(H_B):
                kc = kb_ref[chunk_rows(c), h * hw:(h + 1) * hw]
                for cm in range(2):
                    s_ref[slot, 2 * h + cm] = jnp.dot(kc, qpad_ref[2 * h + cm], preferred_element_type=F32)

        def weigh(c, slot):
            kind = bias_kind(c)
            vt = vb_ref[c]
            for h in range(H_B):
                vh = jnp.concatenate([vt[h * hw:(h + 1) * hw, :], ones], axis=0)
                bias = bias_ref[h, kind]
                for cm in range(2):
                    mp = 2 * h + cm
                    p = jnp.exp2(s_ref[slot, mp] + bias).astype(BF16)
                    acc_ref[mp] += jnp.dot(vh, p, preferred_element_type=F32)

        logits(0, 0)

        def pipelined(c, carry):
            slot = lax.rem(c, 2)
            weigh(c, slot)
            logits(jnp.minimum(c + 1, i), 1 - slot)
            return carry

        lax.fori_loop(0, nch, pipelined, 0)
    else:
        def attend(c, carry):
            kind = bias_kind(c)
            vt = vb_ref[c]
            for h in range(H_B):
                kc = kb_ref[chunk_rows(c), h * hw:(h + 1) * hw]
                vh = vt[h * hw:(h + 1) * hw, :]
                bias = bias_ref[h, kind]
                for cm in range(2):
                    mp = 2 * h + cm
                    s = jnp.dot(kc, qpad_ref[mp], preferred_element_type=F32) + bias
                    m_old = m_ref[mp:mp + 1, :]
                    m_new = jnp.maximum(m_old, jnp.max(s, axis=0, keepdims=True))
                    alpha = jnp.exp2(m_old - m_new)
                    p = jnp.exp2(s - m_new)
                    l_ref[mp:mp + 1, :] = alpha * l_ref[mp:mp + 1, :] + jnp.sum(p, axis=0, keepdims=True)
                    m_ref[mp:mp + 1, :] = m_new
                    pv = jnp.dot(vh, p.astype(BF16), preferred_element_type=F32)
                    acc_ref[mp, 0:hw, :] = acc_ref[mp, 0:hw, :] * alpha + pv
            return carry

        lax.fori_loop(0, nch, attend, 0)

    lam = lam_ref[...]
    outs = []
    for h in range(H_B):
        den0 = acc_ref[2 * h, hw:hw + 1, :] if bounded else l_ref[2 * h:2 * h + 1, :]
        den1 = acc_ref[2 * h + 1, hw:hw + 1, :] if bounded else l_ref[2 * h + 1:2 * h + 2, :]
        o = acc_ref[2 * h, 0:hw, :] / den0 - lam * (acc_ref[2 * h + 1, 0:hw, :] / den1)
        ms = jnp.mean(o * o, axis=0, keepdims=True)
        outs.append(o * lax.rsqrt(ms + EPS) * gsub_ref[...])
    o_ref[...] = jnp.concatenate(outs, axis=0).T.astype(o_ref.dtype)


def _diff_prompt_call(qb_t, kb16, vb_t, bias, lam_row, gsub_col, n, s, bounded):
    nq = s // TILE
    w = H_B * 2 * HD
    return pl.pallas_call(
        functools.partial(_diff_kernel, bounded=bounded),
        grid=(n, nq),
        in_specs=[
            pl.BlockSpec((None, w, TILE), lambda b, i: (b * nq + i, 0, 0)),
            pl.BlockSpec((None, s, w), lambda b, i: (b, 0, 0)),
            pl.BlockSpec((None, nq, w, TILE), lambda b, i: (b, 0, 0, 0)),
            pl.BlockSpec((H_B, 3, TILE, TILE), lambda b, i: (0, 0, 0, 0), pipeline_mode=pl.Buffered(1)),
            pl.BlockSpec((1, TILE), lambda b, i: (0, 0)),
            pl.BlockSpec((2 * HD, 1), lambda b, i: (0, 0)),
        ],
        out_specs=pl.BlockSpec((TILE, w), lambda b, i: (b * nq + i, 0)),
        out_shape=jax.ShapeDtypeStruct((n * s, w), BF16),
        scratch_shapes=[
            pltpu.VMEM((2 * H_B, 2 * HD, TILE), BF16),
            pltpu.VMEM((2 * H_B, TILE), F32),
            pltpu.VMEM((2 * H_B, TILE), F32),
            pltpu.VMEM((2 * H_B, 2 * HD + ONES_ROWS, TILE), F32),
            pltpu.VMEM((2, 2 * H_B, TILE, TILE) if bounded else (1, 1, 8, 128), F32),
        ],
        compiler_params=_cparams(2), name="diff_prompt_bounded" if bounded else "diff_prompt_online",
    )(qb_t, kb16.reshape(n, s, w), vb_t.reshape(n, nq, w, TILE), bias, lam_row, gsub_col)


def _bias_tiles(tbl):
    kk = jnp.arange(TILE, dtype=I32)[:, None]
    qq = jnp.arange(TILE, dtype=I32)[None, :]
    shifted = (tbl - tbl[NUM_BUCKETS - 1][None, :]) * LOG2E
    prev = jnp.transpose(_bias_lookup(shifted, qq + TILE - kk), (2, 0, 1))
    diag = jnp.transpose(_bias_lookup(shifted, qq - kk), (2, 0, 1))
    diag = jnp.where((kk <= qq)[None], diag, MASK_BIAS)
    return jnp.stack([jnp.zeros_like(prev), prev, diag], axis=1).astype(F32)


def _post_kernel(y_ref, oa_ref, ob_ref, woa_ref, wob_ref, gc_ref, wmq_ref, gq_ref, bd_ref,
                 mk_ref, mv_ref, wmo_ref, out_ref):
    y1 = y_ref[...] + jnp.dot(oa_ref[...], woa_ref[...], preferred_element_type=F32) \
        + jnp.dot(ob_ref[...], wob_ref[...], preferred_element_type=F32)
    hn = _rms_rows(y1, gc_ref[...]).astype(BF16)
    q = jnp.dot(hn, wmq_ref[...], preferred_element_type=F32)
    q2 = q * q
    q2_hi = q2.astype(BF16)
    q2_lo = (q2 - q2_hi.astype(F32)).astype(BF16)
    ms = jnp.dot(q2_hi, bd_ref[...], preferred_element_type=F32) + jnp.dot(q2_lo, bd_ref[...], preferred_element_type=F32)
    qn = (q * lax.rsqrt(ms + EPS) * gq_ref[...]).astype(BF16)
    o = jnp.zeros(q.shape, F32)
    for h in range(MEM_H):
        s = jnp.dot(qn, mk_ref[h], preferred_element_type=F32)
        p = jnp.exp(s - jnp.max(s, axis=1, keepdims=True))
        p = p / jnp.sum(p, axis=1, keepdims=True)
        o = o + jnp.dot(p.astype(BF16), mv_ref[h], preferred_element_type=F32)
    out_ref[...] = y1 + jnp.dot(o.astype(BF16), wmo_ref[...], preferred_element_type=F32)


def _post_call(y, oa, ob, w_o16, g_cross, w_mq16, g_mq, mk_pad, mv_pad, w_mo16, tm, tiles_per_batch):
    t_tot, d = y.shape
    wa = H_A * HD
    wq = MEM_H * HD
    n_mem = mk_pad.shape[-1]
    bd = jnp.kron(jnp.eye(MEM_H, dtype=F32), jnp.full((HD, HD), 1.0 / HD, F32)).astype(BF16)
    gq = (jnp.tile(g_mq.astype(F32), MEM_H) * (HD ** -0.5)).reshape(1, wq)
    const = lambda shape: pl.BlockSpec(shape, lambda j: (0,) * len(shape))
    return pl.pallas_call(
        _post_kernel,
        grid=(t_tot // tm,),
        in_specs=[
            pl.BlockSpec((tm, d), lambda j: (j, 0)),
            pl.BlockSpec((tm, wa), lambda j: (j, 0)),
            pl.BlockSpec((tm, wa), lambda j: (j, 0)),
            const((wa, d)), const((wa, d)), const((1, d)), const((d, wq)), const((1, wq)), const((wq, wq)),
            pl.BlockSpec((None, MEM_H, wq, n_mem), lambda j: (j // tiles_per_batch, 0, 0, 0)),
            pl.BlockSpec((None, MEM_H, n_mem, wq), lambda j: (j // tiles_per_batch, 0, 0, 0)),
            const((wq, d)),
        ],
        out_specs=pl.BlockSpec((tm, d), lambda j: (j, 0)),
        out_shape=jax.ShapeDtypeStruct((t_tot, d), F32),
        compiler_params=_cparams(1), name="post_mixer",
    )(y, oa, ob, w_o16[:wa], w_o16[wa:], g_cross.reshape(1, d).astype(F32), w_mq16, gq, bd,
      mk_pad, mv_pad, w_mo16)


def _pad_mem(mk, mv):
    eye = jnp.eye(MEM_H, dtype=mk.dtype)
    mk_pad = jnp.einsum("bshd,hg->bhgds", mk, eye).reshape(mk.shape[0], MEM_H, MEM_H * HD, mk.shape[1])
    mv_pad = jnp.einsum("bshd,hg->bhsgd", mv, eye).reshape(mv.shape[0], MEM_H, mv.shape[1], MEM_H * HD)
    return mk_pad.astype(BF16), mv_pad.astype(BF16)


def _ffn_kernel(x_ref, g_ref, wg_ref, wu_ref, wd_ref, out_ref, hn_ref, acc_ref):
    f = pl.program_id(1)

    @pl.when(f == 0)
    def _():
        x = x_ref[...]
        hn_ref[...] = _rms_rows(x, g_ref[...]).astype(BF16)
        acc_ref[...] = x

    hn = hn_ref[...]
    g = jnp.dot(hn, wg_ref[...], preferred_element_type=F32)
    u = jnp.dot(hn, wu_ref[...], preferred_element_type=F32)
    a = (g / (1.0 + jnp.exp(-g))) * u
    acc_ref[...] += jnp.dot(a.astype(BF16), wd_ref[...], preferred_element_type=F32)

    @pl.when(f == pl.num_programs(1) - 1)
    def _():
        out_ref[...] = acc_ref[...]


def _ffn_call(x, g_ffn, wg16, wu16, wd16, tm):
    t_tot, d = x.shape
    dff = wg16.shape[1]
    nf = 2
    tf = dff // nf
    return pl.pallas_call(
        _ffn_kernel,
        grid=(t_tot // tm, nf),
        in_specs=[
            pl.BlockSpec((tm, d), lambda j, f: (j, 0)),
            pl.BlockSpec((1, d), lambda j, f: (0, 0)),
            pl.BlockSpec((d, tf), lambda j, f: (0, f)),
            pl.BlockSpec((d, tf), lambda j, f: (0, f)),
            pl.BlockSpec((tf, d), lambda j, f: (f, 0)),
        ],
        out_specs=pl.BlockSpec((tm, d), lambda j, f: (j, 0)),
        out_shape=jax.ShapeDtypeStruct((t_tot, d), F32),
        scratch_shapes=[pltpu.VMEM((tm, d), BF16), pltpu.VMEM((tm, d), F32)],
        compiler_params=_cparams(2), name="ffn",
    )(x, g_ffn.reshape(1, d).astype(F32), wg16, wu16, wd16)


def _page_specs(block, layer, n_pages_step):
    def make(p):
        return pl.BlockSpec((None, None) + block,
                            lambda b, j, pt: (layer, pt[b, j * n_pages_step + p]) + (0,) * len(block))
    return [make(p) for p in range(n_pages_step)]


def _idx_score_kernel(pt_ref, iq_ref, iw_ref, iknew_ref, *refs, n_pages_step):
    page_refs = refs[:n_pages_step]
    sc_ref, scn_ref = refs[n_pages_step:]
    iq = iq_ref[...].astype(BF16)
    iw = iw_ref[...]

    def score(keys_t):
        d = jnp.dot(iq, keys_t.astype(BF16), preferred_element_type=F32)
        return jnp.sum(jnp.maximum(d, 0.0) * iw, axis=0, keepdims=True)

    for p in range(n_pages_step):
        sc_ref[:, p * PAGE:(p + 1) * PAGE] = score(page_refs[p][...])

    @pl.when(pl.program_id(1) == 0)
    def _():
        lane = lax.broadcasted_iota(I32, (1, PAGE), 1)
        scn_ref[...] = jnp.where(lane == 0, score(iknew_ref[...]), NEG)


def _idx_score_call(page_table, cache_ik_t, layer, iq, iw_col, iknew_t):
    bsz, n_pages = page_table.shape
    pps = PAGES_PER_STEP
    grid_spec = pltpu.PrefetchScalarGridSpec(
        num_scalar_prefetch=1, grid=(bsz, n_pages // pps),
        in_specs=[
            pl.BlockSpec((None, H_I, D_IDX), lambda b, j, pt: (b, 0, 0)),
            pl.BlockSpec((None, H_I, 1), lambda b, j, pt: (b, 0, 0)),
            pl.BlockSpec((None, D_IDX, PAGE), lambda b, j, pt: (b, 0, 0)),
        ] + _page_specs((D_IDX, PAGE), layer, pps),
        out_specs=[
            pl.BlockSpec((None, 1, pps * PAGE), lambda b, j, pt: (b, 0, j)),
            pl.BlockSpec((None, 1, PAGE), lambda b, j, pt: (b, 0, 0)),
        ])
    return pl.pallas_call(
        functools.partial(_idx_score_kernel, n_pages_step=pps),
        grid_spec=grid_spec,
        out_shape=[jax.ShapeDtypeStruct((bsz, 1, n_pages * PAGE), F32),
                   jax.ShapeDtypeStruct((bsz, 1, PAGE), F32)],
        compiler_params=_cparams(2), name="idx_score_sample",
    )(page_table, iq, iw_col, iknew_t, *([cache_ik_t] * pps))


def _key_to_f32(key):
    bits = jnp.where(key >= 0, key, key ^ jnp.int32(0x7FFFFFFF))
    return pltpu.bitcast(bits, F32)


def _online_softmax_step(s, vals_dot, m_ref, l_ref, acc_ref):
    m_old = m_ref[...]
    m_new = jnp.maximum(m_old, jnp.max(s, axis=1, keepdims=True))
    alpha = jnp.exp2(m_old - m_new)
    p = jnp.exp2(s - m_new)
    l_ref[...] = alpha * l_ref[...] + jnp.sum(p, axis=1, keepdims=True)
    m_ref[...] = m_new
    acc_ref[...] = acc_ref[...] * alpha + vals_dot(p.astype(BF16))


def _dsa_sample_kernel(pt_ref, scfull_ref, sc_ref, bias_ref, biasn_ref, q_ref, knew_ref, vnew_ref, *refs,
                       n_pages_step, topk, past, nbits):
    pps = n_pages_step
    k_refs = refs[:pps]
    v_refs = refs[pps:2 * pps]
    o_ref, tau_ref, jl_ref, m_ref, l_ref, acc_ref = refs[2 * pps:]
    j = pl.program_id(1)

    @pl.when(j == 0)
    def _():
        row = scfull_ref[...]
        nk = row.shape[1]
        kidx = lax.broadcasted_iota(I32, (1, nk), 1)
        kf = jnp.float32(topk)

        def bit_step(it, carry):
            prefix, cge = carry
            cand = prefix | jnp.left_shift(jnp.int32(1), 31 - it)
            thr = _key_to_f32(cand ^ jnp.int32(INT_MIN))[:, 0:1]
            cnt = jnp.sum(jnp.where(row >= thr, 1.0, 0.0), axis=1, keepdims=True)
            take = cnt >= kf
            return jnp.where(take, cand, prefix), jnp.where(take, cnt, cge)

        prefix, cge = lax.fori_loop(0, 32, bit_step, (jnp.zeros((1, PAGE), I32), jnp.zeros((1, PAGE), F32)))
        tau = _key_to_f32(prefix ^ jnp.int32(INT_MIN))
        tau1 = tau[:, 0:1]
        need = kf - jnp.sum(jnp.where(row > tau1, 1.0, 0.0), axis=1, keepdims=True)

        def idx_step(it, jlim):
            cand = jlim | jnp.left_shift(jnp.int32(1), nbits - 1 - it)
            cnt = jnp.sum(jnp.where((row == tau1) & (kidx < cand[:, 0:1]), 1.0, 0.0), axis=1, keepdims=True)
            return jnp.where(cnt < need, cand, jlim)

        jlim = lax.fori_loop(0, nbits, idx_step, jnp.zeros((1, PAGE), I32))
        tau_ref[...] = tau
        jl_ref[...] = jnp.where(cge > kf, jlim, 2 ** 30)
        m_ref[...] = jnp.full(m_ref.shape, NEG, F32)
        l_ref[...] = jnp.zeros(l_ref.shape, F32)
        acc_ref[...] = jnp.zeros(acc_ref.shape, F32)

    tau = tau_ref[:, 0:1]
    jlim = jl_ref[:, 0:1]
    q = q_ref[...].astype(BF16)

    def attend(kt_pages, vt_pages, sc, bias, key0):
        nkeys = sc.shape[1]
        kpos = key0 + lax.broadcasted_iota(I32, (1, nkeys), 1)
        sel = (sc > tau) | ((sc == tau) & (kpos <= jlim))
        s = jnp.concatenate([jnp.dot(q, kt.astype(BF16), preferred_element_type=F32) for kt in kt_pages], axis=1)
        s = jnp.where(sel, s + bias, NEG)

        def vals_dot(p):
            out = None
            for idx, vt in enumerate(vt_pages):
                part = lax.dot_general(p[:, idx * PAGE:(idx + 1) * PAGE], vt.astype(BF16),
                                       (((1,), (1,)), ((), ())), preferred_element_type=F32)
                out = part if out is None else out + part
            return out

        _online_softmax_step(s, vals_dot, m_ref, l_ref, acc_ref)

    attend([r[...] for r in k_refs], [r[...] for r in v_refs], sc_ref[...], bias_ref[...], j * (pps * PAGE))

    @pl.when(j == pl.num_programs(1) - 1)
    def _():
        attend([knew_ref[...]], [vnew_ref[...]], scfull_ref[:, past:past + PAGE], biasn_ref[...], past)
        o_ref[...] = acc_ref[...] / l_ref[...]


def _dsa_sample_call(page_table, cache_kt, cache_vt, layer, sc_all, bias_all, qpad, knew_t, vnew_t, topk):
    bsz, n_pages = page_table.shape
    pps = PAGES_PER_STEP
    past = n_pages * PAGE
    nk = past + PAGE
    w = KV_A * HD
    nbits = max(1, int(math.ceil(math.log2(nk))))
    grid_spec = pltpu.PrefetchScalarGridSpec(
        num_scalar_prefetch=1, grid=(bsz, n_pages // pps),
        in_specs=[
            pl.BlockSpec((None, 1, nk), lambda b, j, pt: (b, 0, 0)),
            pl.BlockSpec((None, 1, pps * PAGE), lambda b, j, pt: (b, 0, j)),
            pl.BlockSpec((H_A, pps * PAGE), lambda b, j, pt: (0, j)),
            pl.BlockSpec((H_A, PAGE), lambda b, j, pt: (0, n_pages)),
            pl.BlockSpec((None, H_A, w), lambda b, j, pt: (b, 0, 0)),
            pl.BlockSpec((None, w, PAGE), lambda b, j, pt: (b, 0, 0)),
            pl.BlockSpec((None, w, PAGE), lambda b, j, pt: (b, 0, 0)),
        ] + _page_specs((w, PAGE), layer, pps) + _page_specs((w, PAGE), layer, pps),
        out_specs=pl.BlockSpec((None, H_A, w), lambda b, j, pt: (b, 0, 0)),
        scratch_shapes=[pltpu.VMEM((1, PAGE), F32), pltpu.VMEM((1, PAGE), I32),
                        pltpu.VMEM((H_A, 1), F32), pltpu.VMEM((H_A, 1), F32), pltpu.VMEM((H_A, w), F32)])
    return pl.pallas_call(
        functools.partial(_dsa_sample_kernel, n_pages_step=pps, topk=topk, past=past, nbits=nbits),
        grid_spec=grid_spec,
        out_shape=jax.ShapeDtypeStruct((bsz, H_A, w), F32),
        compiler_params=_cparams(2), name="dsa_sample",
    )(page_table, sc_all, sc_all, bias_all, bias_all, qpad, knew_t, vnew_t,
      *([cache_kt] * pps), *([cache_vt] * pps))


def _diff_sample_kernel(pt_ref, bias_ref, biasn_ref, q_ref, knew_ref, vnew_ref, lam_ref, gsub_ref, *refs,
                        n_pages_step):
    pps = n_pages_step
    k_refs = refs[:pps]
    v_refs = refs[pps:2 * pps]
    o_ref, m_ref, l_ref, acc_ref = refs[2 * pps:]
    j = pl.program_id(1)
    nm = 2 * H_B
    head_of_row = lax.broadcasted_iota(I32, (nm, PAGE), 0) // 2

    @pl.when(j == 0)
    def _():
        m_ref[...] = jnp.full(m_ref.shape, NEG, F32)
        l_ref[...] = jnp.zeros(l_ref.shape, F32)
        acc_ref[...] = jnp.zeros(acc_ref.shape, F32)

    q = q_ref[...].astype(BF16)

    def head_rows(ref, h):
        return ref[pl.ds(h, PAGE, stride=H_B), :].astype(BF16)

    def attend(kpages, vpages, bias, valid):
        cols = []
        for kp in kpages:
            sp = jnp.zeros((nm, PAGE), F32)
            for h in range(H_B):
                sh = lax.dot_general(q, head_rows(kp, h), (((1,), (1,)), ((), ())), preferred_element_type=F32)
                sp = jnp.where(head_of_row == h, sh, sp)
            cols.append(sp)
        s = jnp.concatenate(cols, axis=1) + bias
        if valid is not None:
            s = jnp.where(valid, s, NEG)

        def vals_dot(p):
            out = jnp.zeros((nm, 2 * HD), F32)
            for idx, vp in enumerate(vpages):
                pp = p[:, idx * PAGE:(idx + 1) * PAGE]
                for h in range(H_B):
                    part = jnp.dot(pp, head_rows(vp, h), preferred_element_type=F32)
                    out = out + jnp.where(head_of_row == h, part, 0.0)
            return out

        _online_softmax_step(s, vals_dot, m_ref, l_ref, acc_ref)

    attend(k_refs, v_refs, bias_ref[...], None)

    @pl.when(j == pl.num_programs(1) - 1)
    def _():
        lane = lax.broadcasted_iota(I32, (1, PAGE), 1)
        attend([knew_ref], [vnew_ref], biasn_ref[...], lane == 0)
        o = acc_ref[...] / l_ref[...]
        lam = lam_ref[...]
        outs = []
        for h in range(H_B):
            oh = o[2 * h:2 * h + 1, :] - lam * o[2 * h + 1:2 * h + 2, :]
            ms = jnp.mean(oh * oh, axis=1, keepdims=True)
            outs.append(oh * lax.rsqrt(ms + EPS) * gsub_ref[...])
        o_ref[...] = jnp.concatenate(outs, axis=1)


def _diff_sample_call(page_table, cache_k, cache_v, layer, bias_all, qmaps, knew_pad, vnew_pad, lam_row, gsub_row):
    bsz, n_pages = page_table.shape
    pps = PAGES_PER_STEP
    hw = 2 * HD
    nm = 2 * H_B
    prow = PAGE * H_B
    grid_spec = pltpu.PrefetchScalarGridSpec(
        num_scalar_prefetch=1, grid=(bsz, n_pages // pps),
        in_specs=[
            pl.BlockSpec((nm, pps * PAGE), lambda b, j, pt: (0, j)),
            pl.BlockSpec((nm, PAGE), lambda b, j, pt: (0, n_pages)),
            pl.BlockSpec((None, nm, hw), lambda b, j, pt: (b, 0, 0)),
            pl.BlockSpec((None, prow, hw), lambda b, j, pt: (b, 0, 0)),
            pl.BlockSpec((None, prow, hw), lambda b, j, pt: (b, 0, 0)),
            pl.BlockSpec((1, hw), lambda b, j, pt: (0, 0)),
            pl.BlockSpec((1, hw), lambda b, j, pt: (0, 0)),
        ] + _page_specs((prow, hw), layer, pps) + _page_specs((prow, hw), layer, pps),
        out_specs=pl.BlockSpec((None, 1, H_B * hw), lambda b, j, pt: (b, 0, 0)),
        scratch_shapes=[pltpu.VMEM((nm, 1), F32), pltpu.VMEM((nm, 1), F32), pltpu.VMEM((nm, hw), F32)])
    return pl.pallas_call(
        functools.partial(_diff_sample_kernel, n_pages_step=pps),
        grid_spec=grid_spec,
        out_shape=jax.ShapeDtypeStruct((bsz, 1, H_B * hw), F32),
        compiler_params=_cparams(2), name="diff_sample",
    )(page_table, bias_all, bias_all, qmaps, knew_pad, vnew_pad, lam_row, gsub_row,
      *([cache_k] * pps), *([cache_v] * pps))


def _first_col_page(x):
    return jnp.pad(x[:, :, None], ((0, 0), (0, 0), (0, PAGE - 1)))


def kernel(x_prompt, x_sample, cache_a_k, cache_a_v, cache_a_ik, cache_b_k, cache_b_v, cache_mem_k, cache_mem_v, page_table, mem_prompt, rel_bias, g_mix, w_in, g_qa, g_ka, g_qb, g_kb, lam, g_sub, w_o, g_cross, g_mem, w_mq, w_mkv, g_mq, g_mk, w_mo, g_ffn, w_gu, w_down):
    n, s, d = x_prompt.shape
    bsz = x_sample.shape[0]
    depth = w_in.shape[0]
    n_mem = mem_prompt.shape[1]
    n_pages = page_table.shape[1]
    past = n_pages * PAGE
    n_pool = cache_a_k.shape[1]
    dff = w_down.shape[1]
    assert x_sample.shape[1] == 1 and s % TILE == 0 and n_pages % PAGES_PER_STEP == 0

    tbl_a = rel_bias[:, :H_A].astype(F32)
    tbl_b = rel_bias[:, H_A:].astype(F32)
    bias_a = _bias_tiles(tbl_a)
    bias_b = _bias_tiles(tbl_b)
    kpos = jnp.arange(past + PAGE, dtype=I32)
    bias_sa = _bias_lookup(tbl_a * LOG2E, past - kpos).T
    bias_sb = jnp.repeat(_bias_lookup(tbl_b * LOG2E, past - kpos).T, 2, axis=0)

    cik_t = jnp.transpose(cache_a_ik, (0, 1, 3, 2))
    ck_at = jnp.transpose(cache_a_k, (0, 1, 3, 4, 2)).reshape(depth, n_pool, KV_A * HD, PAGE)
    cv_at = jnp.transpose(cache_a_v, (0, 1, 3, 4, 2)).reshape(depth, n_pool, KV_A * HD, PAGE)
    ck_b = cache_b_k.reshape(depth, n_pool, PAGE * H_B, 2 * HD)
    cv_b = cache_b_v.reshape(depth, n_pool, PAGE * H_B, 2 * HD)

    srows = bsz * SAMPLE_ROWS
    srows_pad = -(-srows // (2 * TILE)) * (2 * TILE)
    ys = jnp.zeros((srows_pad, d), F32).at[0:srows:SAMPLE_ROWS].set(x_sample[:, 0, :])
    yp = x_prompt.reshape(n * s, d)
    mem2d = mem_prompt.reshape(n * n_mem, d)
    topk_s = min(TOPK_MAX, (past + 1) // 4)

    rows_p = [[] for _ in range(7)]
    rows_s = [[] for _ in range(5)]
    for l in range(depth):
        lam_init = 0.8 - 0.6 * math.exp(-0.3 * l)
        lf = lam[l].astype(F32)
        lam_l = jnp.exp(jnp.sum(lf[0] * lf[1])) - jnp.exp(jnp.sum(lf[2] * lf[3])) + lam_init
        gsub = g_sub[l].astype(F32) * (1.0 - lam_init)
        w_t, gcol = _in_proj_weights(w_in[l], g_qa[l], g_ka[l], g_qb[l], g_kb[l])
        w_o16 = w_o[l].astype(BF16)
        w_mq16 = w_mq[l].astype(BF16)
        w_mo16 = w_mo[l].astype(BF16)
        wg16 = w_gu[l, :, :dff].astype(BF16)
        wu16 = w_gu[l, :, dff:].astype(BF16)
        wd16 = w_down[l].astype(BF16)

        wkv_t = w_mkv[l].T
        wkv_hi = wkv_t.astype(BF16)
        wkv_lo = (wkv_t - wkv_hi.astype(F32)).astype(BF16)
        wq = MEM_H * HD
        gcol_m = jnp.concatenate([jnp.tile(g_mk[l].astype(F32), MEM_H), jnp.ones((wq,), F32)]).reshape(-1, 1)
        mem_segs = ((0, wq, True, (("R32", 0, wq),)), (wq, wq, False, (("R32", 0, wq),)))
        mk, mv = _proj_call(mem2d, g_mem[l], wkv_hi, gcol_m, mem_segs, TILE, w_lo=wkv_lo)
        mk = mk.reshape(n, n_mem, MEM_H, HD)
        mv = mv.reshape(n, n_mem, MEM_H, HD)
        mk_pad, mv_pad = _pad_mem(mk, mv)

        (qa_t, iq_t, qb_t, ka, ka16, kb, kb16, va, va_t, vb, vb_t, ik, ik16, iw_t) = _proj_call(
            yp, g_mix[l], w_t, gcol, _IN_SEGS, TILE)
        dsa_args = (qa_t, iq_t, iw_t, ik16, ka16, va_t, bias_a)
        o_a = lax.cond(_logit_bound_ok(g_qa[l], g_ka[l], tbl_a),
                       lambda *a: _dsa_prompt_call(*a, n, s, True),
                       lambda *a: _dsa_prompt_call(*a, n, s, False), *dsa_args)
        diff_args = (qb_t, kb16, vb_t, bias_b, jnp.full((1, TILE), lam_l, F32), gsub.reshape(-1, 1))
        o_b = lax.cond(_logit_bound_ok(g_qb[l], g_kb[l], tbl_b),
                       lambda *a: _diff_prompt_call(*a, n, s, True),
                       lambda *a: _diff_prompt_call(*a, n, s, False), *diff_args)
        tm_post = min(512, s)
        yp = _post_call(yp, o_a, o_b, w_o16, g_cross[l], w_mq16, g_mq[l], mk_pad, mv_pad, w_mo16,
                        tm_post, s // tm_post)
        yp = _ffn_call(yp, g_ffn[l], wg16, wu16, wd16, min(512, n * s))
        for i, r in enumerate((ka.reshape(n, s, KV_A, HD), va.reshape(n, s, KV_A, HD), ik.reshape(n, s, D_IDX),
                               kb.reshape(n, s, H_B, 2 * HD), vb.reshape(n, s, H_B, 2 * HD), mk, mv)):
            rows_p[i].append(r)

        (sqa_t, siq_t, sqb_t, ska, _, skb, _, sva, _, svb, _, sik, _, siw_t) = _proj_call(
            ys, g_mix[l], w_t, gcol, _IN_SEGS, TILE)

        def rows_of_t(a_t, width):
            a = jnp.transpose(a_t, (0, 2, 1)).reshape(-1, a_t.shape[1])
            return a[0:srows:SAMPLE_ROWS, :width].astype(F32)

        take = lambda a: a[0:srows:SAMPLE_ROWS]
        s_qa = rows_of_t(sqa_t, H_A * HD).reshape(bsz, H_A, HD)
        s_iq = rows_of_t(siq_t, H_I * D_IDX).reshape(bsz, H_I, D_IDX)
        s_qb = rows_of_t(sqb_t, 2 * H_B * HD).reshape(bsz, 2 * H_B, HD)
        s_iw = rows_of_t(siw_t, H_I)
        s_ka, s_va, s_ik, s_kb, s_vb = take(ska), take(sva), take(sik), take(skb), take(svb)

        sc_past, sc_new = _idx_score_call(page_table, cik_t, l, s_iq, s_iw.reshape(bsz, H_I, 1),
                                          _first_col_page(s_ik))
        sc_all = jnp.concatenate([sc_past, sc_new], axis=2)
        grp = jnp.eye(KV_A, dtype=F32)[jnp.arange(H_A) // G_A]
        qpad = (s_qa[:, :, None, :] * grp[None, :, :, None]).reshape(bsz, H_A, KV_A * HD)
        o_sa = _dsa_sample_call(page_table, ck_at, cv_at, l, sc_all, bias_sa, qpad,
                                _first_col_page(s_ka), _first_col_page(s_va), topk_s)
        o_sa = jnp.einsum("bhgd,hg->bhd", o_sa.reshape(bsz, H_A, KV_A, HD), grp).reshape(bsz, H_A * HD)
        half = jnp.eye(2, dtype=F32)[jnp.arange(2 * H_B) % 2]
        qmaps = (s_qb[:, :, None, :] * half[None, :, :, None]).reshape(bsz, 2 * H_B, 2 * HD)
        new_rows = lambda x: jnp.pad(x.reshape(bsz, H_B, 2 * HD), ((0, 0), (0, PAGE * H_B - H_B), (0, 0)))
        o_sb = _diff_sample_call(page_table, ck_b, cv_b, l, bias_sb, qmaps, new_rows(s_kb), new_rows(s_vb),
                                 jnp.full((1, 2 * HD), lam_l, F32), gsub.reshape(1, -1)).reshape(bsz, H_B * 2 * HD)

        def scatter_rows(o):
            return jnp.zeros((srows_pad, o.shape[1]), BF16).at[0:srows:SAMPLE_ROWS].set(o.astype(BF16))

        smk_pad, smv_pad = _pad_mem(cache_mem_k[l], cache_mem_v[l])
        n_fill = srows_pad // SAMPLE_ROWS - bsz
        if n_fill:
            smk_pad = jnp.pad(smk_pad, ((0, n_fill), (0, 0), (0, 0), (0, 0)))
            smv_pad = jnp.pad(smv_pad, ((0, n_fill), (0, 0), (0, 0), (0, 0)))
        ys = _post_call(ys, scatter_rows(o_sa), scatter_rows(o_sb), w_o16, g_cross[l], w_mq16, g_mq[l],
                        smk_pad, smv_pad, w_mo16, SAMPLE_ROWS, 1)
        ys = _ffn_call(ys, g_ffn[l], wg16, wu16, wd16, min(512, srows_pad))
        for i, r in enumerate((s_ka.reshape(bsz, 1, KV_A, HD), s_va.reshape(bsz, 1, KV_A, HD),
                               s_ik.reshape(bsz, 1, D_IDX), s_kb.reshape(bsz, 1, H_B, 2 * HD),
                               s_vb.reshape(bsz, 1, H_B, 2 * HD))):
            rows_s[i].append(r)

    y_sample = ys[0:srows:SAMPLE_ROWS].reshape(bsz, 1, d)
    return (yp.reshape(n, s, d), y_sample,
            jnp.stack(rows_p[0]), jnp.stack(rows_p[1]), jnp.stack(rows_p[2]), jnp.stack(rows_p[3]),
            jnp.stack(rows_p[4]), jnp.stack(rows_p[5]), jnp.stack(rows_p[6]),
            jnp.stack(rows_s[0]), jnp.stack(rows_s[1]), jnp.stack(rows_s[2]), jnp.stack(rows_s[3]),
            jnp.stack(rows_s[4]))
```
